```python
import jax, jax.numpy as jnp
from jax import lax
import numpy as np

D_MODEL = 2048
BATCH = 2
SEQ = 4096
DEPTH = 1

SB_HEADS = 8
SB_HEAD_DIM = 128
SB_WIDTH = SB_HEADS * SB_HEAD_DIM
SB_BLOCK = 128

SSD_D_INNER = 2048
SSD_HEAD_DIM = 64
SSD_HEADS = SSD_D_INNER // SSD_HEAD_DIM
SSD_GROUPS = 8
SSD_HEADS_PER_GROUP = SSD_HEADS // SSD_GROUPS
SSD_STATE = 128
SSD_CONV = 4
SSD_CHUNK = 256
SSD_CONV_CH = SSD_D_INNER + 2 * SSD_GROUPS * SSD_STATE

MEM_LEN = 256
MEM_HEADS = 4
MEM_HEAD_DIM = 256
MEM_WIDTH = MEM_HEADS * MEM_HEAD_DIM

N_BRANCHES = 3
NORM_EPS = 1e-6

IN_SPLITS = (SB_WIDTH, SB_WIDTH, SB_WIDTH, SB_WIDTH,
             SSD_D_INNER, SSD_CONV_CH, SSD_HEADS,
             MEM_WIDTH, MEM_WIDTH,
             N_BRANCHES * D_MODEL)
IN_WIDTH = sum(IN_SPLITS)

kernel_name = 'hybrid_stickbreak_ssd_memory_gated'


def rms_norm(x, w):
    xf = x.astype(jnp.float32)
    y = xf * lax.rsqrt(jnp.mean(xf * xf, axis=-1, keepdims=True) + NORM_EPS)
    return (y * w.astype(jnp.float32)).astype(x.dtype)


def gated_group_rms_norm(y, z, w):
    lead = y.shape[:-1]
    g = y.astype(jnp.float32) * jax.nn.silu(z.astype(jnp.float32))
    g = g.reshape(lead + (SSD_GROUPS, SSD_D_INNER // SSD_GROUPS))
    g = g * lax.rsqrt(jnp.mean(g * g, axis=-1, keepdims=True) + NORM_EPS)
    g = g.reshape(lead + (SSD_D_INNER,)) * w.astype(jnp.float32)
    return g.astype(z.dtype)


def stick_breaking_attention(q, k, v):
    b, s, h, d = q.shape
    nb = s // SB_BLOCK
    scale = d ** -0.5
    q_blocks = q.reshape(b, nb, SB_BLOCK, h, d).transpose(1, 0, 2, 3, 4)
    key_pos = jnp.arange(s)

    def one_block(args):
        q_blk, blk = args
        logits = jnp.einsum('bqhd,bkhd->bhqk', q_blk, k).astype(jnp.float32) * scale
        q_pos = blk * SB_BLOCK + jnp.arange(SB_BLOCK)
        earlier = key_pos[None, :] < q_pos[:, None]
        log_beta = jax.nn.log_sigmoid(logits)
        log_keep = jnp.where(earlier, log_beta - logits, 0.0)
        later = lax.cumsum(log_keep, axis=3, reverse=True) - log_keep
        weights = jnp.where(earlier, jnp.exp(log_beta + later), 0.0)
        return jnp.einsum('bhqk,bkhd->bqhd', weights.astype(v.dtype), v)

    out = lax.map(one_block, (q_blocks, jnp.arange(nb)))
    return out.transpose(1, 0, 2, 3, 4).reshape(b, s, h * d)


def causal_depthwise_conv(u, w, bias):
    ch = u.shape[-1]
    out = lax.conv_general_dilated(
        u, w[:, None, :].astype(u.dtype), window_strides=(1,),
        padding=[(SSD_CONV - 1, 0)], dimension_numbers=('NWC', 'WIO', 'NWC'),
        feature_group_count=ch)
    return out + bias.astype(u.dtype)


def ssd_chunked_scan(x, dt, a, b_in, c_in):
    bsz, s = x.shape[0], x.shape[1]
    pad = (-s) % SSD_CHUNK
    xs = (x.astype(jnp.float32) * dt[..., None])
    a_dt = dt * a
    bm = b_in.astype(jnp.float32)
    cm = c_in.astype(jnp.float32)
    xs = jnp.pad(xs, ((0, 0), (0, pad), (0, 0), (0, 0)))
    a_dt = jnp.pad(a_dt, ((0, 0), (0, pad), (0, 0)))
    bm = jnp.pad(bm, ((0, 0), (0, pad), (0, 0), (0, 0)))
    cm = jnp.pad(cm, ((0, 0), (0, pad), (0, 0), (0, 0)))
    nc = (s + pad) // SSD_CHUNK
    g, e, p, n, l = SSD_GROUPS, SSD_HEADS_PER_GROUP, SSD_HEAD_DIM, SSD_STATE, SSD_CHUNK
    xs = xs.reshape(bsz, nc, l, g, e, p)
    bm = bm.reshape(bsz, nc, l, g, n)
    cm = cm.reshape(bsz, nc, l, g, n)
    a_dt = a_dt.reshape(bsz, nc, l, g, e).transpose(0, 3, 4, 1, 2)
    a_cum = jnp.cumsum(a_dt, axis=-1)
    tri = jnp.tril(jnp.ones((l, l), dtype=bool))
    seg = a_cum[..., :, None] - a_cum[..., None, :]
    decay_in = jnp.exp(jnp.where(tri, seg, -jnp.inf))
    cb = jnp.einsum('bclgn,bcsgn->bcgls', cm, bm)
    y_diag = jnp.einsum('bcgls,bgecls,bcsgep->bclgep', cb, decay_in, xs)
    decay_to_end = jnp.exp(a_cum[..., -1:] - a_cum)
    states = jnp.einsum('bclgn,bgecl,bclgep->bcgepn', bm, decay_to_end, xs)
    states = jnp.concatenate([jnp.zeros_like(states[:, :1]), states], axis=1)
    chunk_cum = jnp.cumsum(jnp.pad(a_cum[..., -1], ((0, 0), (0, 0), (0, 0), (1, 0))), axis=-1)
    tri_c = jnp.tril(jnp.ones((nc + 1, nc + 1), dtype=bool))
    decay_chunk = jnp.exp(jnp.where(tri_c, chunk_cum[..., :, None] - chunk_cum[..., None, :], -jnp.inf))
    states = jnp.einsum('bgezc,bcgepn->bzgepn', decay_chunk, states)[:, :-1]
    y_off = jnp.einsum('bclgn,bcgepn,bgecl->bclgep', cm, states, jnp.exp(a_cum))
    y = (y_diag + y_off).reshape(bsz, nc * l, SSD_HEADS, p)
    return y[:, :s]


def memory_cross_attention(q, mem_n, w_kv):
    bsz, m = mem_n.shape[0], mem_n.shape[1]
    kv = mem_n @ w_kv
    k, v = jnp.split(kv, 2, axis=-1)
    k = k.reshape(bsz, m, MEM_HEADS, MEM_HEAD_DIM)
    v = v.reshape(bsz, m, MEM_HEADS, MEM_HEAD_DIM)
    logits = jnp.einsum('bqhd,bkhd->bhqk', q, k).astype(jnp.float32) * (MEM_HEAD_DIM ** -0.5)
    probs = jax.nn.softmax(logits, axis=-1)
    o = jnp.einsum('bhqk,bkhd->bqhd', probs.astype(v.dtype), v)
    return o.reshape(bsz, q.shape[1], MEM_WIDTH)


def setup_inputs(seed: int = 0) -> dict:
    key = jax.random.key(seed)
    ks = jax.random.split(key, 20)

    def dense(k, fan_in, fan_out):
        return jax.random.normal(k, (DEPTH, fan_in, fan_out), jnp.float32) * fan_in ** -0.5

    def gain(k, dim):
        return 1.0 + 0.02 * jax.random.normal(k, (DEPTH, dim), jnp.float32)

    dt0 = jnp.exp(jax.random.uniform(ks[8], (DEPTH, SSD_HEADS), jnp.float32,
                                     float(np.log(1e-3)), float(np.log(1e-1))))
    dt_bias = dt0 + jnp.log(-jnp.expm1(-dt0))
    a_log = jnp.log(jax.random.uniform(ks[9], (DEPTH, SSD_HEADS), jnp.float32, 1.0, 16.0))
    return {
        'x': jax.random.normal(ks[0], (BATCH, SEQ, D_MODEL), jnp.float32),
        'mem': jax.random.normal(ks[1], (BATCH, MEM_LEN, D_MODEL), jnp.float32),
        'norm_w': gain(ks[2], D_MODEL),
        'mem_norm_w': gain(ks[3], D_MODEL),
        'w_in': dense(ks[4], D_MODEL, IN_WIDTH),
        'b_gate': 0.01 * jax.random.normal(ks[5], (DEPTH, N_BRANCHES * D_MODEL), jnp.float32),
        'conv_w': jax.random.normal(ks[6], (DEPTH, SSD_CONV, SSD_CONV_CH), jnp.float32) * SSD_CONV ** -0.5,
        'conv_b': 0.01 * jax.random.normal(ks[7], (DEPTH, SSD_CONV_CH), jnp.float32),
        'dt_bias': dt_bias,
        'a_log': a_log,
        'd_skip': 1.0 + 0.1 * jax.random.normal(ks[10], (DEPTH, SSD_HEADS), jnp.float32),
        'ssd_norm_w': gain(ks[11], SSD_D_INNER),
        'w_mem_kv': dense(ks[12], D_MODEL, 2 * MEM_WIDTH),
        'w_branch_sb': dense(ks[13], SB_WIDTH, D_MODEL),
        'w_branch_ssd': dense(ks[14], SSD_D_INNER, D_MODEL),
        'w_branch_mem': dense(ks[15], MEM_WIDTH, D_MODEL),
        'w_out': dense(ks[16], D_MODEL, D_MODEL),
        'final_norm_w': 1.0 + 0.02 * jax.random.normal(ks[17], (D_MODEL,), jnp.float32),
    }


def reference(x, mem, norm_w, mem_norm_w, w_in, b_gate, conv_w, conv_b, dt_bias, a_log,
              d_skip, ssd_norm_w, w_mem_kv, w_branch_sb, w_branch_ssd, w_branch_mem,
              w_out, final_norm_w):
    bsz, s, _ = x.shape
    offsets = [int(o) for o in np.cumsum(np.array(IN_SPLITS))[:-1]]
    for layer in range(DEPTH):
        h = rms_norm(x, norm_w[layer])
        proj = h @ w_in[layer]
        (sb_q, sb_k, sb_v, sb_z, ssd_z, ssd_xbc, ssd_dt,
         mem_q, mem_z, gate_pre) = jnp.split(proj, offsets, axis=-1)

        o_sb = stick_breaking_attention(
            sb_q.reshape(bsz, s, SB_HEADS, SB_HEAD_DIM),
            sb_k.reshape(bsz, s, SB_HEADS, SB_HEAD_DIM),
            sb_v.reshape(bsz, s, SB_HEADS, SB_HEAD_DIM))
        o_sb = o_sb * jax.nn.silu(sb_z)

        xbc = jax.nn.silu(causal_depthwise_conv(ssd_xbc, conv_w[layer], conv_b[layer]))
        x_ssm, b_in, c_in = jnp.split(xbc, [SSD_D_INNER, SSD_D_INNER + SSD_GROUPS * SSD_STATE], axis=-1)
        dt = jax.nn.softplus(ssd_dt.astype(jnp.float32) + dt_bias[layer].astype(jnp.float32))
        a = -jnp.exp(a_log[layer].astype(jnp.float32))
        x_h = x_ssm.reshape(bsz, s, SSD_HEADS, SSD_HEAD_DIM)
        y = ssd_chunked_scan(x_h, dt, a,
                             b_in.reshape(bsz, s, SSD_GROUPS, SSD_STATE),
                             c_in.reshape(bsz, s, SSD_GROUPS, SSD_STATE))
        y = y + d_skip[layer].astype(jnp.float32)[:, None] * x_h.astype(jnp.float32)
        o_ssd = gated_group_rms_norm(y.reshape(bsz, s, SSD_D_INNER), ssd_z, ssd_norm_w[layer])

        mem_n = rms_norm(mem, mem_norm_w[layer])
        o_mem = memory_cross_attention(mem_q.reshape(bsz, s, MEM_HEADS, MEM_HEAD_DIM),
                                       mem_n, w_mem_kv[layer])
        o_mem = o_mem * jax.nn.silu(mem_z)

        gates = jax.nn.sigmoid((gate_pre + b_gate[layer]).astype(jnp.float32)).astype(x.dtype)
        g_sb, g_ssd, g_mem = jnp.split(gates, N_BRANCHES, axis=-1)
        merged = (g_sb * (o_sb @ w_branch_sb[layer])
                  + g_ssd * (o_ssd @ w_branch_ssd[layer])
                  + g_mem * (o_mem @ w_branch_mem[layer]))
        x = x + merged @ w_out[layer]
    return rms_norm(x, final_norm_w)
```

```python
import functools

import numpy as np
import jax
import jax.numpy as jnp
from jax import lax
from jax.experimental import pallas as pl
from jax.experimental.pallas import tpu as pltpu

F32 = jnp.float32
BF16 = jnp.bfloat16

D_MODEL = 2048
BATCH = 2
SEQ = 4096
TOKENS = BATCH * SEQ

SB_HEADS = 8
SB_HEAD_DIM = 128
SB_WIDTH = SB_HEADS * SB_HEAD_DIM

SSD_D_INNER = 2048
SSD_HEAD_DIM = 64
SSD_HEADS = 32
SSD_GROUPS = 8
SSD_HEADS_PER_GROUP = 4
SSD_STATE = 128
SSD_CONV = 4
SSD_CHUNK = 256
SSD_GROUP_WIDTH = SSD_HEADS_PER_GROUP * SSD_HEAD_DIM

MEM_LEN = 256
MEM_HEADS = 4
MEM_HEAD_DIM = 256
MEM_WIDTH = MEM_HEADS * MEM_HEAD_DIM

NORM_EPS = 1e-6

COL_SB_Q = 0
COL_SB_K = 1024
COL_SB_V = 2048
COL_SB_Z = 3072
COL_SSD_Z = 4096
COL_SSD_X = 6144
COL_SSD_B = 8192
COL_SSD_C = 9216
COL_DT = 10240
DT_WIDTH = 32
COL_MEM_Q = 10240
COL_MEM_Z = 11264
COL_GATE = 12288
PROJ_WIDTH = 18432

LANES = 128
VMEM_LIMIT = 56 * 1024 * 1024


def _sigmoid(v):
    return 1.0 / (1.0 + jnp.exp(-v))


def _split2(v):
    hi = v.astype(BF16)
    lo = (v - hi.astype(F32)).astype(BF16)
    return hi, lo


def _split3(v):
    hi = v.astype(BF16)
    r = v - hi.astype(F32)
    mid = r.astype(BF16)
    lo = (r - mid.astype(F32)).astype(BF16)
    return hi, mid, lo


def _dot(a, b):
    return jnp.dot(a, b, preferred_element_type=F32)


def _dot_nt(a, b):
    return lax.dot_general(a, b, (((1,), (1,)), ((), ())), preferred_element_type=F32)


NORM_TM = 512


def _norm_in_kernel(x_ref, nw_ref, wdt_ref, h_ref, dt_ref):
    x = x_ref[...]
    ms = jnp.mean(x * x, axis=-1, keepdims=True)
    hb = (x * lax.rsqrt(ms + NORM_EPS) * nw_ref[...]).astype(BF16)
    h_ref[...] = hb
    dt_ref[...] = _dot(hb, wdt_ref[...])


def _norm_in(xt, norm_w, w_dt):
    return pl.pallas_call(
        _norm_in_kernel,
        grid=(TOKENS // NORM_TM,),
        in_specs=[
            pl.BlockSpec((NORM_TM, D_MODEL), lambda i: (i, 0)),
            pl.BlockSpec((1, D_MODEL), lambda i: (0, 0)),
            pl.BlockSpec((D_MODEL, LANES), lambda i: (0, 0)),
        ],
        out_specs=[
            pl.BlockSpec((NORM_TM, D_MODEL), lambda i: (i, 0)),
            pl.BlockSpec((NORM_TM, LANES), lambda i: (i, 0)),
        ],
        out_shape=[
            jax.ShapeDtypeStruct((TOKENS, D_MODEL), BF16),
            jax.ShapeDtypeStruct((TOKENS, LANES), F32),
        ],
        compiler_params=pltpu.CompilerParams(
            dimension_semantics=("parallel",), vmem_limit_bytes=VMEM_LIMIT),
        name="norm_in",
    )(xt, norm_w, w_dt)


PROJ_TM = 1024
PROJ_TN = 1024


def _in_proj_kernel(h_ref, w_ref, o_ref):
    o_ref[...] = _dot(h_ref[...], w_ref[...]).astype(BF16)


def _in_proj(h, w):
    return pl.pallas_call(
        _in_proj_kernel,
        grid=(PROJ_WIDTH // PROJ_TN, TOKENS // PROJ_TM),
        in_specs=[
            pl.BlockSpec((PROJ_TM, D_MODEL), lambda j, i: (i, 0)),
            pl.BlockSpec((D_MODEL, PROJ_TN), lambda j, i: (0, j)),
        ],
        out_specs=pl.BlockSpec((PROJ_TM, PROJ_TN), lambda j, i: (i, j)),
        out_shape=jax.ShapeDtypeStruct((TOKENS, PROJ_WIDTH), BF16),
        compiler_params=pltpu.CompilerParams(
            dimension_semantics=("parallel", "parallel"), vmem_limit_bytes=VMEM_LIMIT),
        name="in_proj",
    )(h, w)


SB_TQ = 256
SB_TK = 256
SB_NQ = SEQ // SB_TQ
SB_SCALE = SB_HEAD_DIM ** -0.5


def _sb_tile(q, kblk, vblk, wcs, carry, acc, mask):
    ny = _dot_nt(q, kblk) * (-SB_SCALE)
    lse = jnp.log(1.0 + jnp.exp(-jnp.abs(ny)))
    lk = jnp.minimum(ny, 0.0) - lse
    lb = lk - ny
    if mask is not None:
        lk = jnp.where(mask, lk, 0.0)
    hi, lo = _split2(lk)
    c1 = _dot(jnp.concatenate([hi[:, LANES:], lo[:, LANES:]], axis=1), wcs)
    c0 = _dot(jnp.concatenate([hi[:, :LANES], lo[:, :LANES]], axis=1), wcs)
    base0 = carry + c1[:, LANES:]
    later = jnp.concatenate([c0[:, :LANES] + base0, c1[:, :LANES] + carry], axis=1)
    w = jnp.exp(lb + later)
    if mask is not None:
        w = jnp.where(mask, w, 0.0)
    acc = acc + _dot(w.astype(BF16), vblk)
    return base0 + c0[:, LANES:], acc


def _sb_attn_kernel(q_ref, k_ref, v_ref, z_ref, wcs_ref, o_ref):
    i = pl.program_id(2)
    q = q_ref[...]
    wcs = wcs_ref[...]
    row = lax.broadcasted_iota(jnp.int32, (SB_TQ, SB_TK), 0)
    col = lax.broadcasted_iota(jnp.int32, (SB_TQ, SB_TK), 1)
    zeros = jnp.zeros((SB_TQ, LANES), F32)

    diag = pl.multiple_of(i * SB_TK, SB_TK)
    carry, acc = _sb_tile(q, k_ref[pl.ds(diag, SB_TK), :], v_ref[pl.ds(diag, SB_TK), :],
                          wcs, zeros, zeros, col < row)

    def body(n, ca):
        start = pl.multiple_of((i - 1 - n) * SB_TK, SB_TK)
        return _sb_tile(q, k_ref[pl.ds(start, SB_TK), :], v_ref[pl.ds(start, SB_TK), :],
                        wcs, ca[0], ca[1], None)

    carry, acc = lax.fori_loop(0, i, body, (carry, acc))
    z = z_ref[...].astype(F32)
    o_ref[...] = (acc * (z * _sigmoid(z))).astype(BF16)


def _sb_cumsum_matrix():
    j = np.arange(2 * LANES)[:, None] % LANES
    s = np.arange(2 * LANES)[None, :]
    m = np.where(s < LANES, (j > s), True)
    return jnp.asarray(m.astype(np.float32), dtype=BF16)


def _sb_attn(proj):
    qb = COL_SB_Q // LANES
    kb = COL_SB_K // LANES
    vb = COL_SB_V // LANES
    zb = COL_SB_Z // LANES
    return pl.pallas_call(
        _sb_attn_kernel,
        grid=(BATCH, SB_HEADS, SB_NQ),
        in_specs=[
            pl.BlockSpec((SB_TQ, LANES), lambda b, h, i: (b * SB_NQ + i, qb + h)),
            pl.BlockSpec((SEQ, LANES), lambda b, h, i: (b, kb + h)),
            pl.BlockSpec((SEQ, LANES), lambda b, h, i: (b, vb + h)),
            pl.BlockSpec((SB_TQ, LANES), lambda b, h, i: (b * SB_NQ + i, zb + h)),
            pl.BlockSpec((2 * LANES, 2 * LANES), lambda b, h, i: (0, 0)),
        ],
        out_specs=pl.BlockSpec((SB_TQ, LANES), lambda b, h, i: (b * SB_NQ + i, h)),
        out_shape=jax.ShapeDtypeStruct((TOKENS, SB_WIDTH), BF16),
        compiler_params=pltpu.CompilerParams(
            dimension_semantics=("parallel", "parallel", "arbitrary"),
            vmem_limit_bytes=VMEM_LIMIT),
        name="sb_attn",
    )(proj, proj, proj, proj, _sb_cumsum_matrix())


SSD_NC = SEQ // SSD_CHUNK
SSD_PAD = 8
SSD_ROWS = SSD_PAD + SSD_CHUNK


def _ssd_kernel(x_ref, b_ref, c_ref, z_ref, dtraw_ref,
                cwx_ref, cwb_ref, cwc_ref, cbx_ref, cbb_ref, cbc_ref,
                dtb_ref, alog_ref, sel_ref, dskip_ref, nw_ref, tril_ref, e4_ref,
                o_ref, xpad, bpad, cpad, state):
    c = pl.program_id(2)
    L = SSD_CHUNK

    @pl.when(c == 0)
    def _():
        xpad[0:SSD_PAD, :] = jnp.zeros((SSD_PAD, SSD_GROUP_WIDTH), F32)
        bpad[0:SSD_PAD, :] = jnp.zeros((SSD_PAD, SSD_STATE), F32)
        cpad[0:SSD_PAD, :] = jnp.zeros((SSD_PAD, SSD_STATE), F32)
        state[...] = jnp.zeros((SSD_STATE, SSD_GROUP_WIDTH), F32)

    @pl.when(c > 0)
    def _():
        xpad[0:SSD_PAD, :] = xpad[L:SSD_ROWS, :]
        bpad[0:SSD_PAD, :] = bpad[L:SSD_ROWS, :]
        cpad[0:SSD_PAD, :] = cpad[L:SSD_ROWS, :]

    xpad[SSD_PAD:SSD_ROWS, :] = x_ref[...].astype(F32)
    bpad[SSD_PAD:SSD_ROWS, :] = b_ref[...].astype(F32)
    cpad[SSD_PAD:SSD_ROWS, :] = c_ref[...].astype(F32)

    def conv_silu(pad, w_ref, bias_ref):
        acc = pad[pl.ds(SSD_PAD - SSD_CONV + 1, L), :] * w_ref[0:1, :] + bias_ref[...]
        for k in range(1, SSD_CONV):
            acc = acc + pad[pl.ds(SSD_PAD - SSD_CONV + 1 + k, L), :] * w_ref[k:k + 1, :]
        return acc * _sigmoid(acc)

    xc = conv_silu(xpad, cwx_ref, cbx_ref)
    bm = conv_silu(bpad, cwb_ref, cbb_ref)
    cm = conv_silu(cpad, cwc_ref, cbc_ref)

    raw = dtraw_ref[...] + dtb_ref[...]
    dt_all = jnp.maximum(raw, 0.0) + jnp.log1p(jnp.exp(-jnp.abs(raw)))
    adt_all = dt_all * (-jnp.exp(alog_ref[...]))
    sel = sel_ref[0]
    dt_g = sum(_dot(p, sel) for p in _split3(dt_all))
    adt_g = sum(_dot(p, sel) for p in _split3(adt_all))
    tril = tril_ref[...]
    a_cum = sum(_dot(tril, p) for p in _split3(adt_g))
    a_cum_t = a_cum.T
    a_last = a_cum[L - 1:L, :]
    e4 = e4_ref[...]
    dt_x = sum(_dot(p, e4) for p in _split2(dt_g))
    ea_x = sum(_dot(p, e4) for p in _split2(jnp.exp(a_cum)))
    dte_x = sum(_dot(p, e4) for p in _split2(jnp.exp(a_last - a_cum)))

    xs = xc * dt_x
    bm_bf = bm.astype(BF16)
    cm_bf = cm.astype(BF16)
    cb = _dot_nt(cm_bf, bm_bf)
    row = lax.broadcasted_iota(jnp.int32, (L, L), 0)
    col = lax.broadcasted_iota(jnp.int32, (L, L), 1)
    causal = col <= row
    lane = lax.broadcasted_iota(jnp.int32, (L, SSD_GROUP_WIDTH), 1)
    y = jnp.zeros((L, SSD_GROUP_WIDTH), F32)
    for e in range(SSD_HEADS_PER_GROUP):
        seg = a_cum[:, e:e + 1] - a_cum_t[e:e + 1, :]
        dec = jnp.where(causal, jnp.exp(seg), 0.0)
        m = (cb * dec).astype(BF16)
        head = (lane >= e * SSD_HEAD_DIM) & (lane < (e + 1) * SSD_HEAD_DIM)
        y = y + _dot(m, jnp.where(head, xs, 0.0).astype(BF16))

    st = state[...]
    y = y + _dot(cm_bf, st.astype(BF16)) * ea_x
    state[...] = st * ea_x[L - 1:L, :] + _dot(bm.T.astype(BF16), (xs * dte_x).astype(BF16))

    y = y + dskip_ref[...] * xc
    z = z_ref[...].astype(F32)
    g = y * (z * _sigmoid(z))
    ms = jnp.mean(g * g, axis=-1, keepdims=True)
    o_ref[...] = (g * lax.rsqrt(ms + NORM_EPS) * nw_ref[...]).astype(BF16)


def _ssd(proj, dt_raw, conv_w, conv_b, dt_bias, a_log, d_skip, ssd_norm_w):
    def pad_heads(p):
        return jnp.pad(p.reshape(1, SSD_HEADS), ((0, 0), (0, LANES - SSD_HEADS)))

    sel = np.zeros((SSD_GROUPS, LANES, LANES), np.float32)
    for g in range(SSD_GROUPS):
        for e in range(SSD_HEADS_PER_GROUP):
            sel[g, g * SSD_HEADS_PER_GROUP + e, e] = 1.0
    e4 = np.zeros((LANES, SSD_GROUP_WIDTH), np.float32)
    for e in range(SSD_HEADS_PER_GROUP):
        e4[e, e * SSD_HEAD_DIM:(e + 1) * SSD_HEAD_DIM] = 1.0
    tril = np.tril(np.ones((SSD_CHUNK, SSD_CHUNK), np.float32))
    dskip_x = jnp.repeat(d_skip, SSD_HEAD_DIM).reshape(1, SSD_D_INNER)

    gw = SSD_GROUP_WIDTH
    xb, zb = COL_SSD_X // gw, COL_SSD_Z // gw
    bb, cb = COL_SSD_B // LANES, COL_SSD_C // LANES
    cwb, cwc = SSD_D_INNER // LANES, SSD_D_INNER // LANES + SSD_GROUPS
    rows = lambda b, g, c: b * SSD_NC + c
    full = lambda b, g, c: (0, 0)
    return pl.pallas_call(
        _ssd_kernel,
        grid=(BATCH, SSD_GROUPS, SSD_NC),
        in_specs=[
            pl.BlockSpec((SSD_CHUNK, gw), lambda b, g, c: (rows(b, g, c), xb + g)),
            pl.BlockSpec((SSD_CHUNK, LANES), lambda b, g, c: (rows(b, g, c), bb + g)),
            pl.BlockSpec((SSD_CHUNK, LANES), lambda b, g, c: (rows(b, g, c), cb + g)),
            pl.BlockSpec((SSD_CHUNK, gw), lambda b, g, c: (rows(b, g, c), zb + g)),
            pl.BlockSpec((SSD_CHUNK, LANES), lambda b, g, c: (rows(b, g, c), 0)),
            pl.BlockSpec((SSD_CONV, gw), lambda b, g, c: (0, g)),
            pl.BlockSpec((SSD_CONV, LANES), lambda b, g, c: (0, cwb + g)),
            pl.BlockSpec((SSD_CONV, LANES), lambda b, g, c: (0, cwc + g)),
            pl.BlockSpec((1, gw), lambda b, g, c: (0, g)),
            pl.BlockSpec((1, LANES), lambda b, g, c: (0, cwb + g)),
            pl.BlockSpec((1, LANES), lambda b, g, c: (0, cwc + g)),
            pl.BlockSpec((1, LANES), full),
            pl.BlockSpec((1, LANES), full),
            pl.BlockSpec((1, LANES, LANES), lambda b, g, c: (g, 0, 0)),
            pl.BlockSpec((1, gw), lambda b, g, c: (0, g)),
            pl.BlockSpec((1, gw), lambda b, g, c: (0, g)),
            pl.BlockSpec((SSD_CHUNK, SSD_CHUNK), full),
            pl.BlockSpec((LANES, gw), full),
        ],
        out_specs=pl.BlockSpec((SSD_CHUNK, gw), lambda b, g, c: (rows(b, g, c), g)),
        out_shape=jax.ShapeDtypeStruct((TOKENS, SSD_D_INNER), BF16),
        scratch_shapes=[
            pltpu.VMEM((SSD_ROWS, gw), F32),
            pltpu.VMEM((SSD_ROWS, SSD_STATE), F32),
            pltpu.VMEM((SSD_ROWS, SSD_STATE), F32),
            pltpu.VMEM((SSD_STATE, gw), F32),
        ],
        compiler_params=pltpu.CompilerParams(
            dimension_semantics=("parallel", "parallel", "arbitrary"),
            vmem_limit_bytes=VMEM_LIMIT),
        name="ssd",
    )(proj, proj, proj, proj, dt_raw,
      conv_w, conv_w, conv_w, conv_b, conv_b, conv_b,
      pad_heads(dt_bias), pad_heads(a_log), jnp.asarray(sel, dtype=BF16),
      dskip_x, ssd_norm_w.reshape(1, SSD_D_INNER),
      jnp.asarray(tril, dtype=BF16), jnp.asarray(e4, dtype=BF16))


MEMKV_TN = 512
MEM_TM = 512


def _mem_kv_kernel(m_ref, nw_ref, w_ref, o_ref):
    m = m_ref[...]
    ms = jnp.mean(m * m, axis=-1, keepdims=True)
    mn = (m * lax.rsqrt(ms + NORM_EPS) * nw_ref[...]).astype(BF16)
    o_ref[...] = _dot(mn, w_ref[...]).astype(BF16)


def _mem_kv(mem2, mem_norm_w, w_kv):
    rows = BATCH * MEM_LEN
    return pl.pallas_call(
        _mem_kv_kernel,
        grid=(2 * MEM_WIDTH // MEMKV_TN,),
        in_specs=[
            pl.BlockSpec((rows, D_MODEL), lambda j: (0, 0)),
            pl.BlockSpec((1, D_MODEL), lambda j: (0, 0)),
            pl.BlockSpec((D_MODEL, MEMKV_TN), lambda j: (0, j)),
        ],
        out_specs=pl.BlockSpec((rows, MEMKV_TN), lambda j: (0, j)),
        out_shape=jax.ShapeDtypeStruct((rows, 2 * MEM_WIDTH), BF16),
        compiler_params=pltpu.CompilerParams(
            dimension_semantics=("parallel",), vmem_limit_bytes=VMEM_LIMIT),
        name="mem_kv",
    )(mem2, mem_norm_w, w_kv)


def _mem_attn_kernel(q_ref, z_ref, kv_ref, o_ref):
    scale = MEM_HEAD_DIM ** -0.5
    for hd in range(MEM_HEADS):
        lo, hi = hd * MEM_HEAD_DIM, (hd + 1) * MEM_HEAD_DIM
        s = _dot_nt(q_ref[:, lo:hi], kv_ref[:, lo:hi]) * scale
        p = jnp.exp(s - jnp.max(s, axis=-1, keepdims=True))
        den = jnp.sum(p, axis=-1, keepdims=True)
        o = _dot(p.astype(BF16), kv_ref[:, MEM_WIDTH + lo:MEM_WIDTH + hi]) / den
        z = z_ref[:, lo:hi].astype(F32)
        o_ref[:, lo:hi] = (o * (z * _sigmoid(z))).astype(BF16)


def _mem_attn(proj, kv):
    nt = SEQ // MEM_TM
    qb, zb = COL_MEM_Q // MEM_WIDTH, COL_MEM_Z // MEM_WIDTH
    return pl.pallas_call(
        _mem_attn_kernel,
        grid=(BATCH, nt),
        in_specs=[
            pl.BlockSpec((MEM_TM, MEM_WIDTH), lambda b, i: (b * nt + i, qb)),
            pl.BlockSpec((MEM_TM, MEM_WIDTH), lambda b, i: (b * nt + i, zb)),
            pl.BlockSpec((MEM_LEN, 2 * MEM_WIDTH), lambda b, i: (b, 0)),
        ],
        out_specs=pl.BlockSpec((MEM_TM, MEM_WIDTH), lambda b, i: (b * nt + i, 0)),
        out_shape=jax.ShapeDtypeStruct((TOKENS, MEM_WIDTH), BF16),
        compiler_params=pltpu.CompilerParams(
            dimension_semantics=("parallel", "parallel"), vmem_limit_bytes=VMEM_LIMIT),
        name="mem_attn",
    )(proj, proj, kv)


MERGE_TM = 256


def _merge_out_kernel(x_ref, osb_ref, ossd_ref, omem_ref, gsb_ref, gssd_ref, gmem_ref, bg_ref,
                      wsb_ref, wssd_ref, wmem_ref, wout_ref, fnw_ref, o_ref):
    def gate(g_ref, k):
        return _sigmoid(g_ref[...].astype(F32) + bg_ref[:, k * D_MODEL:(k + 1) * D_MODEL])

    merged = gate(gsb_ref, 0) * _dot(osb_ref[...], wsb_ref[...])
    merged = merged + gate(gssd_ref, 1) * _dot(ossd_ref[...], wssd_ref[...])
    merged = merged + gate(gmem_ref, 2) * _dot(omem_ref[...], wmem_ref[...])
    y = x_ref[...] + _dot(merged.astype(BF16), wout_ref[...])
    ms = jnp.mean(y * y, axis=-1, keepdims=True)
    o_ref[...] = y * lax.rsqrt(ms + NORM_EPS) * fnw_ref[...]


def _merge_out(xt, o_sb, o_ssd, o_mem, proj, b_gate, w_sb, w_ssd, w_mem, w_out, final_norm_w):
    gb = COL_GATE // D_MODEL
    tile = lambda width: pl.BlockSpec((MERGE_TM, width), lambda i: (i, 0))
    resident = lambda shape: pl.BlockSpec(shape, lambda i: (0, 0), pipeline_mode=pl.Buffered(1))
    return pl.pallas_call(
        _merge_out_kernel,
        grid=(TOKENS // MERGE_TM,),
        in_specs=[
            tile(D_MODEL), tile(SB_WIDTH), tile(SSD_D_INNER), tile(MEM_WIDTH),
            pl.BlockSpec((MERGE_TM, D_MODEL), lambda i: (i, gb)),
            pl.BlockSpec((MERGE_TM, D_MODEL), lambda i: (i, gb + 1)),
            pl.BlockSpec((MERGE_TM, D_MODEL), lambda i: (i, gb + 2)),
            resident((1, 3 * D_MODEL)),
            resident((SB_WIDTH, D_MODEL)), resident((SSD_D_INNER, D_MODEL)),
            resident((MEM_WIDTH, D_MODEL)), resident((D_MODEL, D_MODEL)),
            resident((1, D_MODEL)),
        ],
        out_specs=tile(D_MODEL),
        out_shape=jax.ShapeDtypeStruct((TOKENS, D_MODEL), F32),
        compiler_params=pltpu.CompilerParams(
            dimension_semantics=("parallel",), vmem_limit_bytes=VMEM_LIMIT),
        name="merge_out",
    )(xt, o_sb, o_ssd, o_mem, proj, proj, proj, b_gate, w_sb, w_ssd, w_mem, w_out, final_norm_w)


def kernel(x, mem, norm_w, mem_norm_w, w_in, b_gate, conv_w, conv_b, dt_bias, a_log, d_skip,
           ssd_norm_w, w_mem_kv, w_branch_sb, w_branch_ssd, w_branch_mem, w_out, final_norm_w):
    xt = x.reshape(TOKENS, D_MODEL)
    w = w_in[0]
    w_main = jnp.concatenate([w[:, :COL_DT], w[:, COL_DT + DT_WIDTH:]], axis=1).astype(BF16)
    w_dt = jnp.pad(w[:, COL_DT:COL_DT + DT_WIDTH], ((0, 0), (0, LANES - DT_WIDTH))).astype(BF16)

    h, dt_raw = _norm_in(xt, norm_w[0].reshape(1, D_MODEL), w_dt)
    proj = _in_proj(h, w_main)
    o_sb = _sb_attn(proj)
    o_ssd = _ssd(proj, dt_raw, conv_w[0], conv_b[0].reshape(1, -1), dt_bias[0], a_log[0],
                 d_skip[0], ssd_norm_w[0])
    kv = _mem_kv(mem.reshape(BATCH * MEM_LEN, D_MODEL), mem_norm_w[0].reshape(1, D_MODEL),
                 w_mem_kv[0].astype(BF16))
    o_mem = _mem_attn(proj, kv)
    out = _merge_out(xt, o_sb, o_ssd, o_mem, proj, b_gate[0].reshape(1, -1),
                     w_branch_sb[0].astype(BF16), w_branch_ssd[0].astype(BF16),
                     w_branch_mem[0].astype(BF16), w_out[0].astype(BF16),
                     final_norm_w.reshape(1, D_MODEL))
    return out.reshape(BATCH, SEQ, D_MODEL)
```

```python
import functools

import numpy as np
import jax
import jax.numpy as jnp
from jax import lax
from jax.experimental import pallas as pl
from jax.experimental.pallas import tpu as pltpu

F32 = jnp.float32
BF16 = jnp.bfloat16

D_MODEL = 2048
BATCH = 2
SEQ = 4096
TOKENS = BATCH * SEQ

SB_HEADS = 8
SB_HEAD_DIM = 128
SB_WIDTH = SB_HEADS * SB_HEAD_DIM

SSD_D_INNER = 2048
SSD_HEAD_DIM = 64
SSD_HEADS = 32
SSD_GROUPS = 8
SSD_HEADS_PER_GROUP = 4
SSD_STATE = 128
SSD_CONV = 4
SSD_CHUNK = 256
SSD_GROUP_WIDTH = SSD_HEADS_PER_GROUP * SSD_HEAD_DIM

MEM_LEN = 256
MEM_HEADS = 4
MEM_HEAD_DIM = 256
MEM_WIDTH = MEM_HEADS * MEM_HEAD_DIM

NORM_EPS = 1e-6

COL_SB_Q = 0
COL_SB_K = 1024
COL_SB_V = 2048
COL_SB_Z = 3072
COL_SSD_Z = 4096
COL_SSD_X = 6144
COL_SSD_B = 8192
COL_SSD_C = 9216
COL_DT = 10240
DT_WIDTH = 32
COL_MEM_Q = 10240
COL_MEM_Z = 11264
COL_GATE = 12288
PROJ_WIDTH = 18432

LANES = 128
VMEM_LIMIT = 56 * 1024 * 1024


def _sigmoid(v):
    return 0.5 + 0.5 * jnp.tanh(0.5 * v)


def _silu(v):
    h = 0.5 * v
    return h + h * jnp.tanh(h)


def _split2(v):
    hi = v.astype(BF16)
    lo = (v - hi.astype(F32)).astype(BF16)
    return hi, lo


def _split3(v):
    hi = v.astype(BF16)
    r = v - hi.astype(F32)
    mid = r.astype(BF16)
    lo = (r - mid.astype(F32)).astype(BF16)
    return hi, mid, lo


def _dot(a, b):
    return jnp.dot(a, b, preferred_element_type=F32)


def _dot_nt(a, b):
    return lax.dot_general(a, b, (((1,), (1,)), ((), ())), preferred_element_type=F32)


NORM_TM = 512


def _norm_in_kernel(x_ref, nw_ref, wdt_ref, h_ref, dt_ref):
    x = x_ref[...]
    ms = jnp.mean(x * x, axis=-1, keepdims=True)
    hb = (x * lax.rsqrt(ms + NORM_EPS) * nw_ref[...]).astype(BF16)
    h_ref[...] = hb
    dt_ref[...] = _dot_nt(hb, wdt_ref[...].astype(BF16))


def _norm_in(xt, norm_w, w_in_t):
    return pl.pallas_call(
        _norm_in_kernel,
        grid=(TOKENS // NORM_TM,),
        in_specs=[
            pl.BlockSpec((NORM_TM, D_MODEL), lambda i: (i, 0)),
            pl.BlockSpec((1, D_MODEL), lambda i: (0, 0)),
            pl.BlockSpec((LANES, D_MODEL), lambda i: (COL_DT // LANES, 0)),
        ],
        out_specs=[
            pl.BlockSpec((NORM_TM, D_MODEL), lambda i: (i, 0)),
            pl.BlockSpec((NORM_TM, LANES), lambda i: (i, 0)),
        ],
        out_shape=[
            jax.ShapeDtypeStruct((TOKENS, D_MODEL), BF16),
            jax.ShapeDtypeStruct((TOKENS, LANES), F32),
        ],
        compiler_params=pltpu.CompilerParams(
            dimension_semantics=("parallel",), vmem_limit_bytes=VMEM_LIMIT),
        name="norm_in",
    )(xt, norm_w, w_in_t)


PROJ_TM = 1024
PROJ_TN = 1024


PROJ_SHIFT_TILE = COL_DT // PROJ_TN


def _in_proj_kernel(h_ref, w_ref, wnext_ref, o_ref, wb_ref):
    j = pl.program_id(0)

    @pl.when(pl.program_id(1) == 0)
    def _():
        @pl.when(j < PROJ_SHIFT_TILE)
        def _():
            wb_ref[...] = w_ref[...].astype(BF16)

        @pl.when(j >= PROJ_SHIFT_TILE)
        def _():
            keep = PROJ_TN - DT_WIDTH
            wb_ref[:keep, :] = w_ref[DT_WIDTH:, :].astype(BF16)
            wb_ref[keep:, :] = wnext_ref[...].astype(BF16)

    scale = jnp.where(j == 0, SB_Q_PRESCALE, 1.0).astype(F32)
    o_ref[...] = (_dot_nt(h_ref[...], wb_ref[...]) * scale).astype(BF16)


def _in_proj(h, w_in_t):
    next_per_tile = PROJ_TN // DT_WIDTH
    return pl.pallas_call(
        _in_proj_kernel,
        grid=(PROJ_WIDTH // PROJ_TN, TOKENS // PROJ_TM),
        in_specs=[
            pl.BlockSpec((PROJ_TM, D_MODEL), lambda j, i: (i, 0)),
            pl.BlockSpec((PROJ_TN, D_MODEL), lambda j, i: (j, 0)),
            pl.BlockSpec((DT_WIDTH, D_MODEL), lambda j, i: ((j + 1) * next_per_tile, 0)),
        ],
        out_specs=pl.BlockSpec((PROJ_TM, PROJ_TN), lambda j, i: (i, j)),
        out_shape=jax.ShapeDtypeStruct((TOKENS, PROJ_WIDTH), BF16),
        scratch_shapes=[pltpu.VMEM((PROJ_TN, D_MODEL), BF16)],
        compiler_params=pltpu.CompilerParams(
            dimension_semantics=("parallel", "arbitrary"), vmem_limit_bytes=VMEM_LIMIT),
        name="in_proj",
    )(h, w_in_t, w_in_t)


SB_TQ = 256
SB_TK = 256
SB_NQ = SEQ // SB_TQ
SB_HPS = 8
SB_STEP_WIDTH = SB_HPS * SB_HEAD_DIM
LOG2E = 1.4426950408889634
SB_Q_PRESCALE = -(SB_HEAD_DIM ** -0.5) * LOG2E


def _sb_weights(ny, wcs, carry, mask):
    lse = jnp.log(1.0 + jnp.exp2(-jnp.abs(ny))) * LOG2E
    lk = jnp.minimum(ny, 0.0) - lse
    if mask is not None:
        lk = jnp.where(mask, lk, 0.0)
    hi, lo = _split2(lk)
    c1 = _dot(jnp.concatenate([hi[:, LANES:], lo[:, LANES:]], axis=1), wcs)
    c0 = _dot(jnp.concatenate([hi[:, :LANES], lo[:, :LANES]], axis=1), wcs)
    base0 = carry + c1[:, LANES:]
    zexp = jnp.concatenate([c0[:, :LANES] + base0, c1[:, :LANES] + carry], axis=1) - ny
    w = jnp.exp2(zexp)
    if mask is not None:
        w = jnp.where(mask, w, 0.0)
    return base0 + c0[:, LANES:], w.astype(BF16)


def _sb_attn_kernel(q_ref, k_ref, v_ref, z_ref, wcs_ref, o_ref, carry_ref, acc_ref, ny_ref, w_ref):
    i = pl.program_id(2)
    wcs = wcs_ref[...]
    row = lax.broadcasted_iota(jnp.int32, (SB_TQ, SB_TK), 0)
    col = lax.broadcasted_iota(jnp.int32, (SB_TQ, SB_TK), 1)
    heads = [slice(hh * SB_HEAD_DIM, (hh + 1) * SB_HEAD_DIM) for hh in range(SB_HPS)]

    def block_start(kb):
        return pl.multiple_of(kb * SB_TK, SB_TK)

    def qk(kb):
        for hh, sl in enumerate(heads):
            ny_ref[hh] = _dot_nt(q_ref[:, sl], k_ref[pl.ds(block_start(kb), SB_TK), sl])

    def pv(kb):
        for hh, sl in enumerate(heads):
            acc_ref[hh] += _dot(w_ref[hh], v_ref[pl.ds(block_start(kb), SB_TK), sl])

    def weights(mask):
        for hh in range(SB_HPS):
            carry_ref[hh], w_ref[hh] = _sb_weights(ny_ref[hh], wcs, carry_ref[hh], mask)

    carry_ref[...] = jnp.zeros(carry_ref.shape, F32)
    acc_ref[...] = jnp.zeros(acc_ref.shape, F32)
    qk(i)
    weights(col < row)
    qk(jnp.maximum(i - 1, 0))

    def body(n, _):
        cur = i - 1 - n
        pv(cur + 1)
        weights(None)
        qk(jnp.maximum(cur - 1, 0))
        return 0

    lax.fori_loop(0, i, body, 0)
    pv(0)
    for hh, sl in enumerate(heads):
        z = z_ref[:, sl].astype(F32)
        o_ref[:, sl] = (acc_ref[hh] * _silu(z)).astype(BF16)


def _sb_cumsum_matrix():
    j = np.arange(2 * LANES)[:, None] % LANES
    s = np.arange(2 * LANES)[None, :]
    m = np.where(s < LANES, (j >= s), True)
    return jnp.asarray(m.astype(np.float32), dtype=BF16)


def _sb_attn(proj):
    w = SB_STEP_WIDTH
    qb, kb, vb, zb = COL_SB_Q // w, COL_SB_K // w, COL_SB_V // w, COL_SB_Z // w
    return pl.pallas_call(
        _sb_attn_kernel,
        grid=(BATCH, SB_HEADS // SB_HPS, SB_NQ),
        in_specs=[
            pl.BlockSpec((SB_TQ, w), lambda b, h, i: (b * SB_NQ + i, qb + h)),
            pl.BlockSpec((SEQ, w), lambda b, h, i: (b, kb + h)),
            pl.BlockSpec((SEQ, w), lambda b, h, i: (b, vb + h)),
            pl.BlockSpec((SB_TQ, w), lambda b, h, i: (b * SB_NQ + i, zb + h)),
            pl.BlockSpec((2 * LANES, 2 * LANES), lambda b, h, i: (0, 0)),
        ],
        out_specs=pl.BlockSpec((SB_TQ, w), lambda b, h, i: (b * SB_NQ + i, h)),
        out_shape=jax.ShapeDtypeStruct((TOKENS, SB_WIDTH), BF16),
        scratch_shapes=[
            pltpu.VMEM((SB_HPS, SB_TQ, LANES), F32),
            pltpu.VMEM((SB_HPS, SB_TQ, LANES), F32),
            pltpu.VMEM((SB_HPS, SB_TQ, SB_TK), F32),
            pltpu.VMEM((SB_HPS, SB_TQ, SB_TK), BF16),
        ],
        compiler_params=pltpu.CompilerParams(
            dimension_semantics=("parallel", "parallel", "arbitrary"),
            vmem_limit_bytes=VMEM_LIMIT),
        name="sb_attn",
    )(proj, proj, proj, proj, _sb_cumsum_matrix())


SSD_L = 128
SSD_NC = SEQ // SSD_L
SSD_PAD = 8
SSD_ROWS = SSD_PAD + SSD_L
SSD_XBC = SSD_D_INNER + 2 * SSD_GROUPS * SSD_STATE
SSD_CONV_PIECE = 512


def _ssd_kernel(x_ref, bc_ref, z_ref, dtraw_ref, cw_ref, cbias_ref, dtb_ref, alog_ref,
                dskip_ref, nw_ref, tril_ref, expand_ref,
                o_ref, pad, xc_s, b_s, c_s, state):
    c = pl.program_id(1)
    L = SSD_L

    @pl.when(c == 0)
    def _():
        pad[0:SSD_PAD, :] = jnp.zeros((SSD_PAD, SSD_XBC), F32)
        state[...] = jnp.zeros(state.shape, F32)

    @pl.when(c > 0)
    def _():
        pad[0:SSD_PAD, :] = pad[L:SSD_ROWS, :]

    pad[SSD_PAD:SSD_ROWS, 0:SSD_D_INNER] = x_ref[...].astype(F32)
    pad[SSD_PAD:SSD_ROWS, SSD_D_INNER:SSD_XBC] = bc_ref[...].astype(F32)

    first = SSD_PAD - SSD_CONV + 1
    bc_width = SSD_GROUPS * SSD_STATE
    for p in range(SSD_XBC // SSD_CONV_PIECE):
        lo = p * SSD_CONV_PIECE
        cols = slice(lo, lo + SSD_CONV_PIECE)
        acc = pad[pl.ds(first, L), cols] * cw_ref[0:1, cols] + cbias_ref[:, cols]
        for k in range(1, SSD_CONV):
            acc = acc + pad[pl.ds(first + k, L), cols] * cw_ref[k:k + 1, cols]
        v = _silu(acc)
        if lo < SSD_D_INNER:
            xc_s[:, cols] = v
        elif lo < SSD_D_INNER + bc_width:
            b_s[:, lo - SSD_D_INNER:lo - SSD_D_INNER + SSD_CONV_PIECE] = v.astype(BF16)
        else:
            off = lo - SSD_D_INNER - bc_width
            c_s[:, off:off + SSD_CONV_PIECE] = v.astype(BF16)

    raw = dtraw_ref[...] + dtb_ref[...]
    dt = jnp.maximum(raw, 0.0) + jnp.log1p(jnp.exp(-jnp.abs(raw)))
    adt = dt * (-jnp.exp(alog_ref[...]) * LOG2E)
    tril = tril_ref[...]
    acum = sum(_dot(tril, p) for p in _split3(adt))
    rsrc_t = (acum - jnp.log(dt) * LOG2E).T
    expand = expand_ref[...]
    ea_x = sum(_dot(p, expand) for p in _split2(jnp.exp2(acum)))
    wst = jnp.exp2(acum[L - 1:L, :] - acum) * dt
    wst_x = sum(_dot(p, expand) for p in _split2(wst))

    row = lax.broadcasted_iota(jnp.int32, (L, L), 0)
    col = lax.broadcasted_iota(jnp.int32, (L, L), 1)
    causal = col <= row
    low_half = lax.broadcasted_iota(jnp.int32, (L, LANES), 1) < SSD_HEAD_DIM
    for g in range(SSD_GROUPS):
        gl = slice(g * SSD_GROUP_WIDTH, (g + 1) * SSD_GROUP_WIDTH)
        nl = slice(g * SSD_STATE, (g + 1) * SSD_STATE)
        bg = b_s[:, nl]
        cg = c_s[:, nl]
        cb = _dot_nt(cg, bg)
        xg = xc_s[:, gl]
        halves = []
        for t in range(2):
            xt = xg[:, t * LANES:(t + 1) * LANES]
            acc = None
            for u in range(2):
                hd = g * SSD_HEADS_PER_GROUP + 2 * t + u
                seg = acum[:, hd:hd + 1] - rsrc_t[hd:hd + 1, :]
                m = (cb * jnp.where(causal, jnp.exp2(seg), 0.0)).astype(BF16)
                keep = low_half if u == 0 else jnp.logical_not(low_half)
                d = _dot(m, jnp.where(keep, xt, 0.0).astype(BF16))
                acc = d if acc is None else acc + d
            halves.append(acc)
        y = jnp.concatenate(halves, axis=1)

        st = state[g]
        y = y + _dot(cg, st.astype(BF16)) * ea_x[:, gl]
        state[g] = (st * ea_x[L - 1:L, gl]
                    + _dot(bg.astype(F32).T.astype(BF16), (xg * wst_x[:, gl]).astype(BF16)))

        y = y + dskip_ref[:, gl] * xg
        gated = y * _silu(z_ref[:, gl].astype(F32))
        ms = jnp.mean(gated * gated, axis=-1, keepdims=True)
        o_ref[:, gl] = (gated * lax.rsqrt(ms + NORM_EPS) * nw_ref[:, gl]).astype(BF16)


def _ssd(proj, dt_raw, conv_w, conv_b, dt_bias, a_log, d_skip, ssd_norm_w):
    def pad_heads(p):
        return jnp.pad(p.reshape(1, SSD_HEADS), ((0, 0), (0, LANES - SSD_HEADS)))

    expand = np.zeros((LANES, SSD_D_INNER), np.float32)
    for hd in range(SSD_HEADS):
        expand[hd, hd * SSD_HEAD_DIM:(hd + 1) * SSD_HEAD_DIM] = 1.0
    tril = np.tril(np.ones((SSD_L, SSD_L), np.float32))
    dskip_x = jnp.repeat(d_skip, SSD_HEAD_DIM).reshape(1, SSD_D_INNER)

    w = SSD_D_INNER
    xb, bcb, zb = COL_SSD_X // w, COL_SSD_B // w, COL_SSD_Z // w
    rows = lambda b, c: b * SSD_NC + c
    full = lambda b, c: (0, 0)
    return pl.pallas_call(
        _ssd_kernel,
        grid=(BATCH, SSD_NC),
        in_specs=[
            pl.BlockSpec((SSD_L, w), lambda b, c: (rows(b, c), xb)),
            pl.BlockSpec((SSD_L, w), lambda b, c: (rows(b, c), bcb)),
            pl.BlockSpec((SSD_L, w), lambda b, c: (rows(b, c), zb)),
            pl.BlockSpec((SSD_L, LANES), lambda b, c: (rows(b, c), 0)),
            pl.BlockSpec((SSD_CONV, SSD_XBC), full),
            pl.BlockSpec((1, SSD_XBC), full),
            pl.BlockSpec((1, LANES), full),
            pl.BlockSpec((1, LANES), full),
            pl.BlockSpec((1, w), full),
            pl.BlockSpec((1, w), full),
            pl.BlockSpec((SSD_L, SSD_L), full),
            pl.BlockSpec((LANES, w), full),
        ],
        out_specs=pl.BlockSpec((SSD_L, w), lambda b, c: (rows(b, c), 0)),
        out_shape=jax.ShapeDtypeStruct((TOKENS, SSD_D_INNER), BF16),
        scratch_shapes=[
            pltpu.VMEM((SSD_ROWS, SSD_XBC), F32),
            pltpu.VMEM((SSD_L, SSD_D_INNER), F32),
            pltpu.VMEM((SSD_L, SSD_GROUPS * SSD_STATE), BF16),
            pltpu.VMEM((SSD_L, SSD_GROUPS * SSD_STATE), BF16),
            pltpu.VMEM((SSD_GROUPS, SSD_STATE, SSD_GROUP_WIDTH), F32),
        ],
        compiler_params=pltpu.CompilerParams(
            dimension_semantics=("parallel", "arbitrary"), vmem_limit_bytes=VMEM_LIMIT),
        name="ssd",
    )(proj, proj, proj, dt_raw, conv_w, conv_b,
      pad_heads(dt_bias), pad_heads(a_log), dskip_x, ssd_norm_w.reshape(1, SSD_D_INNER),
      jnp.asarray(tril, dtype=BF16), jnp.asarray(expand, dtype=BF16))


MEMKV_TN = 512
MEM_TM = 512


def _mem_kv_kernel(m_ref, nw_ref, w_ref, o_ref):
    m = m_ref[...]
    ms = jnp.mean(m * m, axis=-1, keepdims=True)
    mn = (m * lax.rsqrt(ms + NORM_EPS) * nw_ref[...]).astype(BF16)
    o_ref[...] = _dot(mn, w_ref[...]).astype(BF16)


def _mem_kv(mem2, mem_norm_w, w_kv):
    rows = BATCH * MEM_LEN
    return pl.pallas_call(
        _mem_kv_kernel,
        grid=(2 * MEM_WIDTH // MEMKV_TN,),
        in_specs=[
            pl.BlockSpec((rows, D_MODEL), lambda j: (0, 0)),
            pl.BlockSpec((1, D_MODEL), lambda j: (0, 0)),
            pl.BlockSpec((D_MODEL, MEMKV_TN), lambda j: (0, j)),
        ],
        out_specs=pl.BlockSpec((rows, MEMKV_TN), lambda j: (0, j)),
        out_shape=jax.ShapeDtypeStruct((rows, 2 * MEM_WIDTH), BF16),
        compiler_params=pltpu.CompilerParams(
            dimension_semantics=("parallel",), vmem_limit_bytes=VMEM_LIMIT),
        name="mem_kv",
    )(mem2, mem_norm_w, w_kv)


def _mem_attn_kernel(q_ref, z_ref, kv_ref, o_ref):
    scale = MEM_HEAD_DIM ** -0.5
    for hd in range(MEM_HEADS):
        lo, hi = hd * MEM_HEAD_DIM, (hd + 1) * MEM_HEAD_DIM
        s = _dot_nt(q_ref[:, lo:hi], kv_ref[:, lo:hi]) * scale
        p = jnp.exp(s - jnp.max(s, axis=-1, keepdims=True))
        den = jnp.sum(p, axis=-1, keepdims=True)
        o = _dot(p.astype(BF16), kv_ref[:, MEM_WIDTH + lo:MEM_WIDTH + hi]) / den
        z = z_ref[:, lo:hi].astype(F32)
        o_ref[:, lo:hi] = (o * _silu(z)).astype(BF16)


def _mem_attn(proj, kv):
    nt = SEQ // MEM_TM
    qb, zb = COL_MEM_Q // MEM_WIDTH, COL_MEM_Z // MEM_WIDTH
    return pl.pallas_call(
        _mem_attn_kernel,
        grid=(BATCH, nt),
        in_specs=[
            pl.BlockSpec((MEM_TM, MEM_WIDTH), lambda b, i: (b * nt + i, qb)),
            pl.BlockSpec((MEM_TM, MEM_WIDTH), lambda b, i: (b * nt + i, zb)),
            pl.BlockSpec((MEM_LEN, 2 * MEM_WIDTH), lambda b, i: (b, 0)),
        ],
        out_specs=pl.BlockSpec((MEM_TM, MEM_WIDTH), lambda b, i: (b * nt + i, 0)),
        out_shape=jax.ShapeDtypeStruct((TOKENS, MEM_WIDTH), BF16),
        compiler_params=pltpu.CompilerParams(
            dimension_semantics=("parallel", "parallel"), vmem_limit_bytes=VMEM_LIMIT),
        name="mem_attn",
    )(proj, proj, kv)


MERGE_TM = 256


def _merge_out_kernel(x_ref, osb_ref, ossd_ref, omem_ref, gsb_ref, gssd_ref, gmem_ref, bg_ref,
                      wsb_ref, wssd_ref, wmem_ref, wout_ref, fnw_ref, o_ref):
    def gate(g_ref, k):
        return _sigmoid(g_ref[...].astype(F32) + bg_ref[:, k * D_MODEL:(k + 1) * D_MODEL])

    merged = gate(gsb_ref, 0) * _dot(osb_ref[...], wsb_ref[...])
    merged = merged + gate(gssd_ref, 1) * _dot(ossd_ref[...], wssd_ref[...])
    merged = merged + gate(gmem_ref, 2) * _dot(omem_ref[...], wmem_ref[...])
    y = x_ref[...] + _dot(merged.astype(BF16), wout_ref[...])
    ms = jnp.mean(y * y, axis=-1, keepdims=True)
    o_ref[...] = y * lax.rsqrt(ms + NORM_EPS) * fnw_ref[...]


def _merge_out(xt, o_sb, o_ssd, o_mem, proj, b_gate, w_sb, w_ssd, w_mem, w_out, final_norm_w):
    gb = COL_GATE // D_MODEL
    tile = lambda width: pl.BlockSpec((MERGE_TM, width), lambda i: (i, 0))
    resident = lambda shape: pl.BlockSpec(shape, lambda i: (0, 0), pipeline_mode=pl.Buffered(1))
    return pl.pallas_call(
        _merge_out_kernel,
        grid=(TOKENS // MERGE_TM,),
        in_specs=[
            tile(D_MODEL), tile(SB_WIDTH), tile(SSD_D_INNER), tile(MEM_WIDTH),
            pl.BlockSpec((MERGE_TM, D_MODEL), lambda i: (i, gb)),
            pl.BlockSpec((MERGE_TM, D_MODEL), lambda i: (i, gb + 1)),
            pl.BlockSpec((MERGE_TM, D_MODEL), lambda i: (i, gb + 2)),
            resident((1, 3 * D_MODEL)),
            resident((SB_WIDTH, D_MODEL)), resident((SSD_D_INNER, D_MODEL)),
            resident((MEM_WIDTH, D_MODEL)), resident((D_MODEL, D_MODEL)),
            resident((1, D_MODEL)),
        ],
        out_specs=tile(D_MODEL),
        out_shape=jax.ShapeDtypeStruct((TOKENS, D_MODEL), F32),
        compiler_params=pltpu.CompilerParams(
            dimension_semantics=("parallel",), vmem_limit_bytes=VMEM_LIMIT),
        name="merge_out",
    )(xt, o_sb, o_ssd, o_mem, proj, proj, proj, b_gate, w_sb, w_ssd, w_mem, w_out, final_norm_w)


def kernel(x, mem, norm_w, mem_norm_w, w_in, b_gate, conv_w, conv_b, dt_bias, a_log, d_skip,
           ssd_norm_w, w_mem_kv, w_branch_sb, w_branch_ssd, w_branch_mem, w_out, final_norm_w):
    xt = x.reshape(TOKENS, D_MODEL)
    w_in_t = w_in[0].T

    h, dt_raw = _norm_in(xt, norm_w[0].reshape(1, D_MODEL), w_in_t)
    proj = _in_proj(h, w_in_t)
    o_sb = _sb_attn(proj)
    o_ssd = _ssd(proj, dt_raw, conv_w[0], conv_b[0].reshape(1, -1), dt_bias[0], a_log[0],
                 d_skip[0], ssd_norm_w[0])
    kv = _mem_kv(mem.reshape(BATCH * MEM_LEN, D_MODEL), mem_norm_w[0].reshape(1, D_MODEL),
                 w_mem_kv[0].astype(BF16))
    o_mem = _mem_attn(proj, kv)
    out = _merge_out(xt, o_sb, o_ssd, o_mem, proj, b_gate[0].reshape(1, -1),
                     w_branch_sb[0].astype(BF16), w_branch_ssd[0].astype(BF16),
                     w_branch_mem[0].astype(BF16), w_out[0].astype(BF16),
                     final_norm_w.reshape(1, D_MODEL))
    return out.reshape(BATCH, SEQ, D_MODEL)
```

```python
import functools

import numpy as np
import jax
import jax.numpy as jnp
from jax import lax
from jax.experimental import pallas as pl
from jax.experimental.pallas import tpu as pltpu

F32 = jnp.float32
BF16 = jnp.bfloat16

D_MODEL = 2048
BATCH = 2
SEQ = 4096
TOKENS = BATCH * SEQ

SB_HEADS = 8
SB_HEAD_DIM = 128
SB_WIDTH = SB_HEADS * SB_HEAD_DIM

SSD_D_INNER = 2048
SSD_HEAD_DIM = 64
SSD_HEADS = 32
SSD_GROUPS = 8
SSD_HEADS_PER_GROUP = 4
SSD_STATE = 128
SSD_CONV = 4
SSD_CHUNK = 256
SSD_GROUP_WIDTH = SSD_HEADS_PER_GROUP * SSD_HEAD_DIM

MEM_LEN = 256
MEM_HEADS = 4
MEM_HEAD_DIM = 256
MEM_WIDTH = MEM_HEADS * MEM_HEAD_DIM

NORM_EPS = 1e-6

COL_SB_Q = 0
COL_SB_K = 1024
COL_SB_V = 2048
COL_SB_Z = 3072
COL_SSD_Z = 4096
COL_SSD_X = 6144
COL_SSD_B = 8192
COL_SSD_C = 9216
COL_DT = 10240
DT_WIDTH = 32
COL_MEM_Q = 10240
COL_MEM_Z = 11264
COL_GATE = 12288
PROJ_WIDTH = 18432

LANES = 128
VMEM_LIMIT = 56 * 1024 * 1024


def _sigmoid(v):
    return 0.5 + 0.5 * jnp.tanh(0.5 * v)


def _silu(v):
    h = 0.5 * v
    return h + h * jnp.tanh(h)


def _split2(v):
    hi = v.astype(BF16)
    lo = (v - hi.astype(F32)).astype(BF16)
    return hi, lo


def _split3(v):
    hi = v.astype(BF16)
    r = v - hi.astype(F32)
    mid = r.astype(BF16)
    lo = (r - mid.astype(F32)).astype(BF16)
    return hi, mid, lo


def _dot(a, b):
    return jnp.dot(a, b, preferred_element_type=F32)


def _dot_nt(a, b):
    return lax.dot_general(a, b, (((1,), (1,)), ((), ())), preferred_element_type=F32)


NORM_TM = 512


def _norm_in_kernel(x_ref, nw_ref, wdt_ref, h_ref, dt_ref):
    x = x_ref[...]
    ms = jnp.mean(x * x, axis=-1, keepdims=True)
    hb = (x * lax.rsqrt(ms + NORM_EPS) * nw_ref[...]).astype(BF16)
    h_ref[...] = hb
    dt_ref[...] = _dot_nt(hb, wdt_ref[...].astype(BF16))


def _norm_in(xt, norm_w, w_in_t):
    return pl.pallas_call(
        _norm_in_kernel,
        grid=(TOKENS // NORM_TM,),
        in_specs=[
            pl.BlockSpec((NORM_TM, D_MODEL), lambda i: (i, 0)),
            pl.BlockSpec((1, D_MODEL), lambda i: (0, 0)),
            pl.BlockSpec((LANES, D_MODEL), lambda i: (COL_DT // LANES, 0)),
        ],
        out_specs=[
            pl.BlockSpec((NORM_TM, D_MODEL), lambda i: (i, 0)),
            pl.BlockSpec((NORM_TM, LANES), lambda i: (i, 0)),
        ],
        out_shape=[
            jax.ShapeDtypeStruct((TOKENS, D_MODEL), BF16),
            jax.ShapeDtypeStruct((TOKENS, LANES), F32),
        ],
        compiler_params=pltpu.CompilerParams(
            dimension_semantics=("parallel",), vmem_limit_bytes=VMEM_LIMIT),
        name="norm_in",
    )(xt, norm_w, w_in_t)


PROJ_TM = 1024
PROJ_TN = 1024


PROJ_SHIFT_TILE = COL_DT // PROJ_TN


def _in_proj_kernel(h_ref, w_ref, wnext_ref, o_ref, wb_ref):
    j = pl.program_id(0)

    @pl.when(pl.program_id(1) == 0)
    def _():
        @pl.when(j < PROJ_SHIFT_TILE)
        def _():
            wb_ref[...] = w_ref[...].astype(BF16)

        @pl.when(j >= PROJ_SHIFT_TILE)
        def _():
            keep = PROJ_TN - DT_WIDTH
            wb_ref[:keep, :] = w_ref[DT_WIDTH:, :].astype(BF16)
            wb_ref[keep:, :] = wnext_ref[...].astype(BF16)

    scale = jnp.where(j == 0, SB_Q_PRESCALE, 1.0).astype(F32)
    o_ref[...] = (_dot_nt(h_ref[...], wb_ref[...]) * scale).astype(BF16)


def _in_proj(h, w_in_t):
    next_per_tile = PROJ_TN // DT_WIDTH
    return pl.pallas_call(
        _in_proj_kernel,
        grid=(PROJ_WIDTH // PROJ_TN, TOKENS // PROJ_TM),
        in_specs=[
            pl.BlockSpec((PROJ_TM, D_MODEL), lambda j, i: (i, 0)),
            pl.BlockSpec((PROJ_TN, D_MODEL), lambda j, i: (j, 0)),
            pl.BlockSpec((DT_WIDTH, D_MODEL), lambda j, i: ((j + 1) * next_per_tile, 0)),
        ],
        out_specs=pl.BlockSpec((PROJ_TM, PROJ_TN), lambda j, i: (i, j)),
        out_shape=jax.ShapeDtypeStruct((TOKENS, PROJ_WIDTH), BF16),
        scratch_shapes=[pltpu.VMEM((PROJ_TN, D_MODEL), BF16)],
        compiler_params=pltpu.CompilerParams(
            dimension_semantics=("parallel", "arbitrary"), vmem_limit_bytes=VMEM_LIMIT),
        name="in_proj",
    )(h, w_in_t, w_in_t)


SB_TQ = 256
SB_TK = 256
SB_NQ = SEQ // SB_TQ
SB_HPS = 8
SB_STEP_WIDTH = SB_HPS * SB_HEAD_DIM
LOG2E = 1.4426950408889634
SB_Q_PRESCALE = -(SB_HEAD_DIM ** -0.5) * LOG2E
SB_DEAD_LOG2 = -160.0


def _sb_weights(ny, wcs, carry, mask):
    lse = jnp.log(1.0 + jnp.exp2(-jnp.abs(ny))) * LOG2E
    lk = jnp.minimum(ny, 0.0) - lse
    if mask is not None:
        lk = jnp.where(mask, lk, 0.0)
    hi, lo = _split2(lk)
    c1 = _dot(jnp.concatenate([hi[:, LANES:], lo[:, LANES:]], axis=1), wcs)
    c0 = _dot(jnp.concatenate([hi[:, :LANES], lo[:, :LANES]], axis=1), wcs)
    base0 = carry + c1[:, LANES:]
    zexp = jnp.concatenate([c0[:, :LANES] + base0, c1[:, :LANES] + carry], axis=1) - ny
    w = jnp.exp2(zexp)
    if mask is not None:
        w = jnp.where(mask, w, 0.0)
    return base0 + c0[:, LANES:], w.astype(BF16)


def _sb_attn_kernel(q_ref, k_ref, v_ref, z_ref, wcs_ref, o_ref, carry_ref, acc_ref, ny_ref, w_ref):
    i = pl.program_id(2)
    wcs = wcs_ref[...]
    row = lax.broadcasted_iota(jnp.int32, (SB_TQ, SB_TK), 0)
    col = lax.broadcasted_iota(jnp.int32, (SB_TQ, SB_TK), 1)
    heads = [slice(hh * SB_HEAD_DIM, (hh + 1) * SB_HEAD_DIM) for hh in range(SB_HPS)]

    def block_start(kb):
        return pl.multiple_of(kb * SB_TK, SB_TK)

    def qk(kb):
        for hh, sl in enumerate(heads):
            ny_ref[hh] = _dot_nt(q_ref[:, sl], k_ref[pl.ds(block_start(kb), SB_TK), sl])

    def pv(kb):
        for hh, sl in enumerate(heads):
            acc_ref[hh] += _dot(w_ref[hh], v_ref[pl.ds(block_start(kb), SB_TK), sl])

    def weights(mask):
        for hh in range(SB_HPS):
            carry_ref[hh], w_ref[hh] = _sb_weights(ny_ref[hh], wcs, carry_ref[hh], mask)

    def more_blocks(cur):
        top = carry_ref[0]
        for hh in range(1, SB_HPS):
            top = jnp.maximum(top, carry_ref[hh])
        return jnp.logical_and(cur > 0, jnp.max(top) > SB_DEAD_LOG2)

    carry_ref[...] = jnp.zeros(carry_ref.shape, F32)
    acc_ref[...] = jnp.zeros(acc_ref.shape, F32)
    qk(i)
    weights(col < row)
    qk(jnp.maximum(i - 1, 0))

    def body(state):
        last, _ = state
        cur = last - 1
        pv(last)
        weights(None)
        qk(jnp.maximum(cur - 1, 0))
        return cur, more_blocks(cur)

    last, _ = lax.while_loop(lambda state: state[1], body, (i, more_blocks(i)))
    pv(last)
    for hh, sl in enumerate(heads):
        z = z_ref[:, sl].astype(F32)
        o_ref[:, sl] = (acc_ref[hh] * _silu(z)).astype(BF16)


def _sb_cumsum_matrix():
    j = np.arange(2 * LANES)[:, None] % LANES
    s = np.arange(2 * LANES)[None, :]
    m = np.where(s < LANES, (j >= s), True)
    return jnp.asarray(m.astype(np.float32), dtype=BF16)


def _sb_attn(proj):
    w = SB_STEP_WIDTH
    qb, kb, vb, zb = COL_SB_Q // w, COL_SB_K // w, COL_SB_V // w, COL_SB_Z // w
    return pl.pallas_call(
        _sb_attn_kernel,
        grid=(BATCH, SB_HEADS // SB_HPS, SB_NQ),
        in_specs=[
            pl.BlockSpec((SB_TQ, w), lambda b, h, i: (b * SB_NQ + i, qb + h)),
            pl.BlockSpec((SEQ, w), lambda b, h, i: (b, kb + h)),
            pl.BlockSpec((SEQ, w), lambda b, h, i: (b, vb + h)),
            pl.BlockSpec((SB_TQ, w), lambda b, h, i: (b * SB_NQ + i, zb + h)),
            pl.BlockSpec((2 * LANES, 2 * LANES), lambda b, h, i: (0, 0)),
        ],
        out_specs=pl.BlockSpec((SB_TQ, w), lambda b, h, i: (b * SB_NQ + i, h)),
        out_shape=jax.ShapeDtypeStruct((TOKENS, SB_WIDTH), BF16),
        scratch_shapes=[
            pltpu.VMEM((SB_HPS, SB_TQ, LANES), F32),
            pltpu.VMEM((SB_HPS, SB_TQ, LANES), F32),
            pltpu.VMEM((SB_HPS, SB_TQ, SB_TK), F32),
            pltpu.VMEM((SB_HPS, SB_TQ, SB_TK), BF16),
        ],
        compiler_params=pltpu.CompilerParams(
            dimension_semantics=("parallel", "parallel", "arbitrary"),
            vmem_limit_bytes=VMEM_LIMIT),
        name="sb_attn",
    )(proj, proj, proj, proj, _sb_cumsum_matrix())


SSD_L = 128
SSD_NC = SEQ // SSD_L
SSD_PAD = 8
SSD_ROWS = SSD_PAD + SSD_L
SSD_XBC = SSD_D_INNER + 2 * SSD_GROUPS * SSD_STATE
SSD_CONV_PIECE = 512


def _ssd_kernel(x_ref, bc_ref, z_ref, dtraw_ref, cw_ref, cbias_ref, dtb_ref, alog_ref,
                dskip_ref, nw_ref, tril_ref, expand_ref,
                o_ref, pad, xc_s, b_s, c_s, state):
    c = pl.program_id(1)
    L = SSD_L

    @pl.when(c == 0)
    def _():
        pad[0:SSD_PAD, :] = jnp.zeros((SSD_PAD, SSD_XBC), F32)
        state[...] = jnp.zeros(state.shape, F32)

    @pl.when(c > 0)
    def _():
        pad[0:SSD_PAD, :] = pad[L:SSD_ROWS, :]

    pad[SSD_PAD:SSD_ROWS, 0:SSD_D_INNER] = x_ref[...].astype(F32)
    pad[SSD_PAD:SSD_ROWS, SSD_D_INNER:SSD_XBC] = bc_ref[...].astype(F32)

    bc_width = SSD_GROUPS * SSD_STATE
    for p in range(SSD_XBC // SSD_CONV_PIECE):
        lo = p * SSD_CONV_PIECE
        cols = slice(lo, lo + SSD_CONV_PIECE)
        u = pad[:, cols]
        u1 = pltpu.roll(u, 1, 0)
        older = u1 * cw_ref[0:1, cols] + u * cw_ref[1:2, cols]
        newer = u1 * cw_ref[2:3, cols] + u * cw_ref[3:4, cols]
        acc = pltpu.roll(older, 2, 0) + newer
        v = _silu(acc[SSD_PAD:SSD_ROWS, :] + cbias_ref[:, cols])
        if lo < SSD_D_INNER:
            xc_s[:, cols] = v
        elif lo < SSD_D_INNER + bc_width:
            b_s[:, lo - SSD_D_INNER:lo - SSD_D_INNER + SSD_CONV_PIECE] = v.astype(BF16)
        else:
            off = lo - SSD_D_INNER - bc_width
            c_s[:, off:off + SSD_CONV_PIECE] = v.astype(BF16)

    raw = dtraw_ref[...] + dtb_ref[...]
    dt = jnp.maximum(raw, 0.0) + jnp.log1p(jnp.exp(-jnp.abs(raw)))
    adt = dt * (-jnp.exp(alog_ref[...]) * LOG2E)
    tril = tril_ref[...]
    acum = sum(_dot(tril, p) for p in _split3(adt))
    rsrc_t = (acum - jnp.log(dt) * LOG2E).T
    expand = expand_ref[...]
    ea_x = sum(_dot(p, expand) for p in _split2(jnp.exp2(acum)))
    wst = jnp.exp2(acum[L - 1:L, :] - acum) * dt
    wst_x = sum(_dot(p, expand) for p in _split2(wst))

    row = lax.broadcasted_iota(jnp.int32, (L, L), 0)
    col = lax.broadcasted_iota(jnp.int32, (L, L), 1)
    causal = col <= row
    low_half = lax.broadcasted_iota(jnp.int32, (L, LANES), 1) < SSD_HEAD_DIM
    for g in range(SSD_GROUPS):
        gl = slice(g * SSD_GROUP_WIDTH, (g + 1) * SSD_GROUP_WIDTH)
        nl = slice(g * SSD_STATE, (g + 1) * SSD_STATE)
        bg = b_s[:, nl]
        cg = c_s[:, nl]
        cb = _dot_nt(cg, bg)
        xg = xc_s[:, gl]
        halves = []
        for t in range(2):
            xt = xg[:, t * LANES:(t + 1) * LANES]
            acc = None
            for u in range(2):
                hd = g * SSD_HEADS_PER_GROUP + 2 * t + u
                seg = acum[:, hd:hd + 1] - rsrc_t[hd:hd + 1, :]
                m = (cb * jnp.where(causal, jnp.exp2(seg), 0.0)).astype(BF16)
                keep = low_half if u == 0 else jnp.logical_not(low_half)
                d = _dot(m, jnp.where(keep, xt, 0.0).astype(BF16))
                acc = d if acc is None else acc + d
            halves.append(acc)
        y = jnp.concatenate(halves, axis=1)

        st = state[g]
        y = y + _dot(cg, st.astype(BF16)) * ea_x[:, gl]
        state[g] = (st * ea_x[L - 1:L, gl]
                    + _dot(bg.astype(F32).T.astype(BF16), (xg * wst_x[:, gl]).astype(BF16)))

        y = y + dskip_ref[:, gl] * xg
        gated = y * _silu(z_ref[:, gl].astype(F32))
        ms = jnp.mean(gated * gated, axis=-1, keepdims=True)
        o_ref[:, gl] = (gated * lax.rsqrt(ms + NORM_EPS) * nw_ref[:, gl]).astype(BF16)


def _ssd(proj, dt_raw, conv_w, conv_b, dt_bias, a_log, d_skip, ssd_norm_w):
    def pad_heads(p):
        return jnp.pad(p.reshape(1, SSD_HEADS), ((0, 0), (0, LANES - SSD_HEADS)))

    expand = np.zeros((LANES, SSD_D_INNER), np.float32)
    for hd in range(SSD_HEADS):
        expand[hd, hd * SSD_HEAD_DIM:(hd + 1) * SSD_HEAD_DIM] = 1.0
    tril = np.tril(np.ones((SSD_L, SSD_L), np.float32))
    dskip_x = jnp.repeat(d_skip, SSD_HEAD_DIM).reshape(1, SSD_D_INNER)

    w = SSD_D_INNER
    xb, bcb, zb = COL_SSD_X // w, COL_SSD_B // w, COL_SSD_Z // w
    rows = lambda b, c: b * SSD_NC + c
    full = lambda b, c: (0, 0)
    return pl.pallas_call(
        _ssd_kernel,
        grid=(BATCH, SSD_NC),
        in_specs=[
            pl.BlockSpec((SSD_L, w), lambda b, c: (rows(b, c), xb)),
            pl.BlockSpec((SSD_L, w), lambda b, c: (rows(b, c), bcb)),
            pl.BlockSpec((SSD_L, w), lambda b, c: (rows(b, c), zb)),
            pl.BlockSpec((SSD_L, LANES), lambda b, c: (rows(b, c), 0)),
            pl.BlockSpec((SSD_CONV, SSD_XBC), full),
            pl.BlockSpec((1, SSD_XBC), full),
            pl.BlockSpec((1, LANES), full),
            pl.BlockSpec((1, LANES), full),
            pl.BlockSpec((1, w), full),
            pl.BlockSpec((1, w), full),
            pl.BlockSpec((SSD_L, SSD_L), full),
            pl.BlockSpec((LANES, w), full),
        ],
        out_specs=pl.BlockSpec((SSD_L, w), lambda b, c: (rows(b, c), 0)),
        out_shape=jax.ShapeDtypeStruct((TOKENS, SSD_D_INNER), BF16),
        scratch_shapes=[
            pltpu.VMEM((SSD_ROWS, SSD_XBC), F32),
            pltpu.VMEM((SSD_L, SSD_D_INNER), F32),
            pltpu.VMEM((SSD_L, SSD_GROUPS * SSD_STATE), BF16),
            pltpu.VMEM((SSD_L, SSD_GROUPS * SSD_STATE), BF16),
            pltpu.VMEM((SSD_GROUPS, SSD_STATE, SSD_GROUP_WIDTH), F32),
        ],
        compiler_params=pltpu.CompilerParams(
            dimension_semantics=("parallel", "arbitrary"), vmem_limit_bytes=VMEM_LIMIT),
        name="ssd",
    )(proj, proj, proj, dt_raw, conv_w, conv_b,
      pad_heads(dt_bias), pad_heads(a_log), dskip_x, ssd_norm_w.reshape(1, SSD_D_INNER),
      jnp.asarray(tril, dtype=BF16), jnp.asarray(expand, dtype=BF16))


MEMKV_TN = 512
MEM_TM = 512


def _mem_kv_kernel(m_ref, nw_ref, w_ref, o_ref):
    m = m_ref[...]
    ms = jnp.mean(m * m, axis=-1, keepdims=True)
    mn = (m * lax.rsqrt(ms + NORM_EPS) * nw_ref[...]).astype(BF16)
    o_ref[...] = _dot(mn, w_ref[...]).astype(BF16)


def _mem_kv(mem2, mem_norm_w, w_kv):
    rows = BATCH * MEM_LEN
    return pl.pallas_call(
        _mem_kv_kernel,
        grid=(2 * MEM_WIDTH // MEMKV_TN,),
        in_specs=[
            pl.BlockSpec((rows, D_MODEL), lambda j: (0, 0)),
            pl.BlockSpec((1, D_MODEL), lambda j: (0, 0)),
            pl.BlockSpec((D_MODEL, MEMKV_TN), lambda j: (0, j)),
        ],
        out_specs=pl.BlockSpec((rows, MEMKV_TN), lambda j: (0, j)),
        out_shape=jax.ShapeDtypeStruct((rows, 2 * MEM_WIDTH), BF16),
        compiler_params=pltpu.CompilerParams(
            dimension_semantics=("parallel",), vmem_limit_bytes=VMEM_LIMIT),
        name="mem_kv",
    )(mem2, mem_norm_w, w_kv)


def _mem_attn_kernel(q_ref, z_ref, kv_ref, o_ref):
    scale = MEM_HEAD_DIM ** -0.5
    for hd in range(MEM_HEADS):
        lo, hi = hd * MEM_HEAD_DIM, (hd + 1) * MEM_HEAD_DIM
        s = _dot_nt(q_ref[:, lo:hi], kv_ref[:, lo:hi]) * scale
        p = jnp.exp(s - jnp.max(s, axis=-1, keepdims=True))
        den = jnp.sum(p, axis=-1, keepdims=True)
        o = _dot(p.astype(BF16), kv_ref[:, MEM_WIDTH + lo:MEM_WIDTH + hi]) / den
        z = z_ref[:, lo:hi].astype(F32)
        o_ref[:, lo:hi] = (o * _silu(z)).astype(BF16)


def _mem_attn(proj, kv):
    nt = SEQ // MEM_TM
    qb, zb = COL_MEM_Q // MEM_WIDTH, COL_MEM_Z // MEM_WIDTH
    return pl.pallas_call(
        _mem_attn_kernel,
        grid=(BATCH, nt),
        in_specs=[
            pl.BlockSpec((MEM_TM, MEM_WIDTH), lambda b, i: (b * nt + i, qb)),
            pl.BlockSpec((MEM_TM, MEM_WIDTH), lambda b, i: (b * nt + i, zb)),
            pl.BlockSpec((MEM_LEN, 2 * MEM_WIDTH), lambda b, i: (b, 0)),
        ],
        out_specs=pl.BlockSpec((MEM_TM, MEM_WIDTH), lambda b, i: (b * nt + i, 0)),
        out_shape=jax.ShapeDtypeStruct((TOKENS, MEM_WIDTH), BF16),
        compiler_params=pltpu.CompilerParams(
            dimension_semantics=("parallel", "parallel"), vmem_limit_bytes=VMEM_LIMIT),
        name="mem_attn",
    )(proj, proj, kv)


MERGE_TM = 256


def _merge_out_kernel(x_ref, osb_ref, ossd_ref, omem_ref, gsb_ref, gssd_ref, gmem_ref, bg_ref,
                      wsb_ref, wssd_ref, wmem_ref, wout_ref, fnw_ref, o_ref):
    def gate(g_ref, k):
        return _sigmoid(g_ref[...].astype(F32) + bg_ref[:, k * D_MODEL:(k + 1) * D_MODEL])

    merged = gate(gsb_ref, 0) * _dot(osb_ref[...], wsb_ref[...])
    merged = merged + gate(gssd_ref, 1) * _dot(ossd_ref[...], wssd_ref[...])
    merged = merged + gate(gmem_ref, 2) * _dot(omem_ref[...], wmem_ref[...])
    y = x_ref[...] + _dot(merged.astype(BF16), wout_ref[...])
    ms = jnp.mean(y * y, axis=-1, keepdims=True)
    o_ref[...] = y * lax.rsqrt(ms + NORM_EPS) * fnw_ref[...]


def _merge_out(xt, o_sb, o_ssd, o_mem, proj, b_gate, w_sb, w_ssd, w_mem, w_out, final_norm_w):
    gb = COL_GATE // D_MODEL
    tile = lambda width: pl.BlockSpec((MERGE_TM, width), lambda i: (i, 0))
    resident = lambda shape: pl.BlockSpec(shape, lambda i: (0, 0), pipeline_mode=pl.Buffered(1))
    return pl.pallas_call(
        _merge_out_kernel,
        grid=(TOKENS // MERGE_TM,),
        in_specs=[
            tile(D_MODEL), tile(SB_WIDTH), tile(SSD_D_INNER), tile(MEM_WIDTH),
            pl.BlockSpec((MERGE_TM, D_MODEL), lambda i: (i, gb)),
            pl.BlockSpec((MERGE_TM, D_MODEL), lambda i: (i, gb + 1)),
            pl.BlockSpec((MERGE_TM, D_MODEL), lambda i: (i, gb + 2)),
            resident((1, 3 * D_MODEL)),
            resident((SB_WIDTH, D_MODEL)), resident((SSD_D_INNER, D_MODEL)),
            resident((MEM_WIDTH, D_MODEL)), resident((D_MODEL, D_MODEL)),
            resident((1, D_MODEL)),
        ],
        out_specs=tile(D_MODEL),
        out_shape=jax.ShapeDtypeStruct((TOKENS, D_MODEL), F32),
        compiler_params=pltpu.CompilerParams(
            dimension_semantics=("parallel",), vmem_limit_bytes=VMEM_LIMIT),
        name="merge_out",
    )(xt, o_sb, o_ssd, o_mem, proj, proj, proj, b_gate, w_sb, w_ssd, w_mem, w_out, final_norm_w)


def kernel(x, mem, norm_w, mem_norm_w, w_in, b_gate, conv_w, conv_b, dt_bias, a_log, d_skip,
           ssd_norm_w, w_mem_kv, w_branch_sb, w_branch_ssd, w_branch_mem, w_out, final_norm_w):
    xt = x.reshape(TOKENS, D_MODEL)
    w_in_t = w_in[0].T

    h, dt_raw = _norm_in(xt, norm_w[0].reshape(1, D_MODEL), w_in_t)
    proj = _in_proj(h, w_in_t)
    o_sb = _sb_attn(proj)
    o_ssd = _ssd(proj, dt_raw, conv_w[0], conv_b[0].reshape(1, -1), dt_bias[0], a_log[0],
                 d_skip[0], ssd_norm_w[0])
    kv = _mem_kv(mem.reshape(BATCH * MEM_LEN, D_MODEL), mem_norm_w[0].reshape(1, D_MODEL),
                 w_mem_kv[0].astype(BF16))
    o_mem = _mem_attn(proj, kv)
    out = _merge_out(xt, o_sb, o_ssd, o_mem, proj, b_gate[0].reshape(1, -1),
                     w_branch_sb[0].astype(BF16), w_branch_ssd[0].astype(BF16),
                     w_branch_mem[0].astype(BF16), w_out[0].astype(BF16),
                     final_norm_w.reshape(1, D_MODEL))
    return out.reshape(BATCH, SEQ, D_MODEL)
```

```python
import numpy as np
import jax
import jax.numpy as jnp
from jax import lax
from jax.experimental import pallas as pl
from jax.experimental.pallas import tpu as pltpu

F32 = jnp.float32
BF16 = jnp.bfloat16

D_MODEL = 2048
BATCH = 2
SEQ = 4096
TOKENS = BATCH * SEQ

SB_HEADS = 8
SB_HEAD_DIM = 128
SB_WIDTH = SB_HEADS * SB_HEAD_DIM

SSD_D_INNER = 2048
SSD_HEAD_DIM = 64
SSD_HEADS = 32
SSD_GROUPS = 8
SSD_HEADS_PER_GROUP = 4
SSD_STATE = 128
SSD_CONV = 4
SSD_CHUNK = 256
SSD_GROUP_WIDTH = SSD_HEADS_PER_GROUP * SSD_HEAD_DIM

MEM_LEN = 256
MEM_HEADS = 4
MEM_HEAD_DIM = 256
MEM_WIDTH = MEM_HEADS * MEM_HEAD_DIM

NORM_EPS = 1e-6

COL_SB_Q = 0
COL_SB_K = 1024
COL_SB_V = 2048
COL_SB_Z = 3072
COL_SSD_Z = 4096
COL_SSD_X = 6144
COL_SSD_B = 8192
COL_SSD_C = 9216
COL_DT = 10240
DT_WIDTH = 32
COL_MEM_Q = 10240
COL_MEM_Z = 11264
COL_GATE = 12288
PROJ_WIDTH = 18432

LANES = 128
VMEM_LIMIT = 56 * 1024 * 1024


def _sigmoid(v):
    return 0.5 + 0.5 * jnp.tanh(0.5 * v)


def _silu(v):
    h = 0.5 * v
    return h + h * jnp.tanh(h)


def _split2(v):
    hi = v.astype(BF16)
    lo = (v - hi.astype(F32)).astype(BF16)
    return hi, lo


def _split3(v):
    hi = v.astype(BF16)
    r = v - hi.astype(F32)
    mid = r.astype(BF16)
    lo = (r - mid.astype(F32)).astype(BF16)
    return hi, mid, lo


def _dot(a, b):
    return jnp.dot(a, b, preferred_element_type=F32)


def _dot_nt(a, b):
    return lax.dot_general(a, b, (((1,), (1,)), ((), ())), preferred_element_type=F32)


NORM_TM = 512


def _norm_in_kernel(x_ref, nw_ref, wdt_ref, h_ref, dt_ref):
    x = x_ref[...]
    ms = jnp.mean(x * x, axis=-1, keepdims=True)
    hb = (x * lax.rsqrt(ms + NORM_EPS) * nw_ref[...]).astype(BF16)
    h_ref[...] = hb
    dt_ref[...] = _dot_nt(hb, wdt_ref[...].astype(BF16))


def _norm_in(xt, norm_w, w_in_t):
    return pl.pallas_call(
        _norm_in_kernel,
        grid=(TOKENS // NORM_TM,),
        in_specs=[
            pl.BlockSpec((NORM_TM, D_MODEL), lambda i: (i, 0)),
            pl.BlockSpec((1, D_MODEL), lambda i: (0, 0)),
            pl.BlockSpec((LANES, D_MODEL), lambda i: (COL_DT // LANES, 0)),
        ],
        out_specs=[
            pl.BlockSpec((NORM_TM, D_MODEL), lambda i: (i, 0)),
            pl.BlockSpec((NORM_TM, LANES), lambda i: (i, 0)),
        ],
        out_shape=[
            jax.ShapeDtypeStruct((TOKENS, D_MODEL), BF16),
            jax.ShapeDtypeStruct((TOKENS, LANES), F32),
        ],
        compiler_params=pltpu.CompilerParams(
            dimension_semantics=("parallel",), vmem_limit_bytes=VMEM_LIMIT),
        name="norm_in",
    )(xt, norm_w, w_in_t)


PROJ_TM = 2048
PROJ_TN = 1024


PROJ_SHIFT_TILE = COL_DT // PROJ_TN


PROJ_CAST_SLICES = 64
PROJ_VMEM_LIMIT = 60 * 1024 * 1024


def _in_proj_kernel(h_ref, w_ref, wnext_ref, *rest):
    n_side = (len(rest) - 2) // 2
    side_in, o_ref, side_out, wb_ref = rest[:n_side], rest[n_side], rest[n_side + 1:-1], rest[-1]
    for src, dst in zip(side_in, side_out):
        dst[...] = src[...].astype(BF16)

    j = pl.program_id(0)

    @pl.when(pl.program_id(1) == 0)
    def _():
        @pl.when(j < PROJ_SHIFT_TILE)
        def _():
            wb_ref[...] = w_ref[...].astype(BF16)

        @pl.when(j >= PROJ_SHIFT_TILE)
        def _():
            keep = PROJ_TN - DT_WIDTH
            wb_ref[:keep, :] = w_ref[DT_WIDTH:, :].astype(BF16)
            wb_ref[keep:, :] = wnext_ref[...].astype(BF16)

    scale = jnp.where(j == 0, SB_Q_PRESCALE, 1.0).astype(F32)
    o_ref[...] = (_dot_nt(h_ref[...], wb_ref[...]) * scale).astype(BF16)


def _in_proj(h, w_in_t, side_weights):
    next_per_tile = PROJ_TN // DT_WIDTH
    steps_i = TOKENS // PROJ_TM
    grid = (PROJ_WIDTH // PROJ_TN, steps_i)
    assert grid[0] * grid[1] >= PROJ_CAST_SLICES
    slice_map = lambda j, i: (jnp.minimum(j * steps_i + i, PROJ_CAST_SLICES - 1), 0)
    side_specs = [pl.BlockSpec((w.shape[0] // PROJ_CAST_SLICES, D_MODEL), slice_map)
                  for w in side_weights]
    outs = pl.pallas_call(
        _in_proj_kernel,
        grid=grid,
        in_specs=[
            pl.BlockSpec((PROJ_TM, D_MODEL), lambda j, i: (i, 0)),
            pl.BlockSpec((PROJ_TN, D_MODEL), lambda j, i: (j, 0)),
            pl.BlockSpec((DT_WIDTH, D_MODEL), lambda j, i: ((j + 1) * next_per_tile, 0)),
        ] + side_specs,
        out_specs=[pl.BlockSpec((PROJ_TM, PROJ_TN), lambda j, i: (i, j))] + side_specs,
        out_shape=[jax.ShapeDtypeStruct((TOKENS, PROJ_WIDTH), BF16)]
        + [jax.ShapeDtypeStruct(w.shape, BF16) for w in side_weights],
        scratch_shapes=[pltpu.VMEM((PROJ_TN, D_MODEL), BF16)],
        compiler_params=pltpu.CompilerParams(
            dimension_semantics=("arbitrary", "arbitrary"), vmem_limit_bytes=PROJ_VMEM_LIMIT),
        name="in_proj",
    )(h, w_in_t, w_in_t, *side_weights)
    return outs[0], outs[1:]


SB_TQ = 256
SB_TK = 256
SB_NQ = SEQ // SB_TQ
SB_HPS = 8
SB_STEP_WIDTH = SB_HPS * SB_HEAD_DIM
LOG2E = 1.4426950408889634
SB_Q_PRESCALE = -(SB_HEAD_DIM ** -0.5) * LOG2E
SB_DEAD_LOG2 = -160.0


def _sb_weights(ny, wcs, carry, mask):
    lse = jnp.log(1.0 + jnp.exp2(-jnp.abs(ny))) * LOG2E
    lk = jnp.minimum(ny, 0.0) - lse
    if mask is not None:
        lk = jnp.where(mask, lk, 0.0)
    hi, lo = _split2(lk)
    c1 = _dot(jnp.concatenate([hi[:, LANES:], lo[:, LANES:]], axis=1), wcs)
    c0 = _dot(jnp.concatenate([hi[:, :LANES], lo[:, :LANES]], axis=1), wcs)
    base0 = carry + c1[:, LANES:]
    zexp = jnp.concatenate([c0[:, :LANES] + base0, c1[:, :LANES] + carry], axis=1) - ny
    w = jnp.exp2(zexp)
    if mask is not None:
        w = jnp.where(mask, w, 0.0)
    return base0 + c0[:, LANES:], w.astype(BF16)


def _sb_attn_kernel(q_ref, k_ref, v_ref, z_ref, wcs_ref, o_ref, carry_ref, acc_ref, ny_ref, w_ref):
    i = pl.program_id(2)
    wcs = wcs_ref[...]
    row = lax.broadcasted_iota(jnp.int32, (SB_TQ, SB_TK), 0)
    col = lax.broadcasted_iota(jnp.int32, (SB_TQ, SB_TK), 1)
    heads = [slice(hh * SB_HEAD_DIM, (hh + 1) * SB_HEAD_DIM) for hh in range(SB_HPS)]

    def block_start(kb):
        return pl.multiple_of(kb * SB_TK, SB_TK)

    def qk(kb):
        for hh, sl in enumerate(heads):
            ny_ref[hh] = _dot_nt(q_ref[:, sl], k_ref[pl.ds(block_start(kb), SB_TK), sl])

    def pv(kb):
        for hh, sl in enumerate(heads):
            acc_ref[hh] += _dot(w_ref[hh], v_ref[pl.ds(block_start(kb), SB_TK), sl])

    def weights(mask):
        for hh in range(SB_HPS):
            carry_ref[hh], w_ref[hh] = _sb_weights(ny_ref[hh], wcs, carry_ref[hh], mask)

    def more_blocks(cur):
        top = carry_ref[0]
        for hh in range(1, SB_HPS):
            top = jnp.maximum(top, carry_ref[hh])
        return jnp.logical_and(cur > 0, jnp.max(top) > SB_DEAD_LOG2)

    carry_ref[...] = jnp.zeros(carry_ref.shape, F32)
    acc_ref[...] = jnp.zeros(acc_ref.shape, F32)
    qk(i)
    weights(col < row)
    qk(jnp.maximum(i - 1, 0))

    def body(state):
        last, _ = state
        cur = last - 1
        pv(last)
        weights(None)
        qk(jnp.maximum(cur - 1, 0))
        return cur, more_blocks(cur)

    last, _ = lax.while_loop(lambda state: state[1], body, (i, more_blocks(i)))
    pv(last)
    for hh, sl in enumerate(heads):
        z = z_ref[:, sl].astype(F32)
        o_ref[:, sl] = (acc_ref[hh] * _silu(z)).astype(BF16)


def _sb_cumsum_matrix():
    j = np.arange(2 * LANES)[:, None] % LANES
    s = np.arange(2 * LANES)[None, :]
    m = np.where(s < LANES, (j >= s), True)
    return jnp.asarray(m.astype(np.float32), dtype=BF16)


def _sb_attn(proj):
    w = SB_STEP_WIDTH
    qb, kb, vb, zb = COL_SB_Q // w, COL_SB_K // w, COL_SB_V // w, COL_SB_Z // w
    return pl.pallas_call(
        _sb_attn_kernel,
        grid=(BATCH, SB_HEADS // SB_HPS, SB_NQ),
        in_specs=[
            pl.BlockSpec((SB_TQ, w), lambda b, h, i: (b * SB_NQ + i, qb + h)),
            pl.BlockSpec((SEQ, w), lambda b, h, i: (b, kb + h)),
            pl.BlockSpec((SEQ, w), lambda b, h, i: (b, vb + h)),
            pl.BlockSpec((SB_TQ, w), lambda b, h, i: (b * SB_NQ + i, zb + h)),
            pl.BlockSpec((2 * LANES, 2 * LANES), lambda b, h, i: (0, 0)),
        ],
        out_specs=pl.BlockSpec((SB_TQ, w), lambda b, h, i: (b * SB_NQ + i, h)),
        out_shape=jax.ShapeDtypeStruct((TOKENS, SB_WIDTH), BF16),
        scratch_shapes=[
            pltpu.VMEM((SB_HPS, SB_TQ, LANES), F32),
            pltpu.VMEM((SB_HPS, SB_TQ, LANES), F32),
            pltpu.VMEM((SB_HPS, SB_TQ, SB_TK), F32),
            pltpu.VMEM((SB_HPS, SB_TQ, SB_TK), BF16),
        ],
        compiler_params=pltpu.CompilerParams(
            dimension_semantics=("parallel", "parallel", "arbitrary"),
            vmem_limit_bytes=VMEM_LIMIT),
        name="sb_attn",
    )(proj, proj, proj, proj, _sb_cumsum_matrix())


SSD_L = 128
SSD_NC = SEQ // SSD_L
SSD_PAD = 8
SSD_ROWS = SSD_PAD + SSD_L
SSD_XBC = SSD_D_INNER + 2 * SSD_GROUPS * SSD_STATE
SSD_CONV_PIECE = 512


def _ssd_kernel(x_ref, bc_ref, z_ref, dtraw_ref, cw_ref, cbias_ref, dtb_ref, alog_ref,
                dskip_ref, nw_ref, tril_ref, expand_ref,
                o_ref, pad, xc_s, b_s, c_s, state):
    c = pl.program_id(1)
    L = SSD_L

    @pl.when(c == 0)
    def _():
        pad[0:SSD_PAD, :] = jnp.zeros((SSD_PAD, SSD_XBC), F32)
        state[...] = jnp.zeros(state.shape, F32)

    @pl.when(c > 0)
    def _():
        pad[0:SSD_PAD, :] = pad[L:SSD_ROWS, :]

    pad[SSD_PAD:SSD_ROWS, 0:SSD_D_INNER] = x_ref[...].astype(F32)
    pad[SSD_PAD:SSD_ROWS, SSD_D_INNER:SSD_XBC] = bc_ref[...].astype(F32)

    bc_width = SSD_GROUPS * SSD_STATE
    for p in range(SSD_XBC // SSD_CONV_PIECE):
        lo = p * SSD_CONV_PIECE
        cols = slice(lo, lo + SSD_CONV_PIECE)
        wh = 0.5 * cw_ref[:, cols]
        u = pad[:, cols]
        u1 = pltpu.roll(u, 1, 0)
        older = u1 * wh[0:1, :] + u * wh[1:2, :]
        newer = u1 * wh[2:3, :] + u * wh[3:4, :]
        acc = pltpu.roll(older, 2, 0) + newer
        h = acc[SSD_PAD:SSD_ROWS, :] + 0.5 * cbias_ref[:, cols]
        v = h + h * jnp.tanh(h)
        if lo < SSD_D_INNER:
            xc_s[:, cols] = v
        elif lo < SSD_D_INNER + bc_width:
            b_s[:, lo - SSD_D_INNER:lo - SSD_D_INNER + SSD_CONV_PIECE] = v.astype(BF16)
        else:
            off = lo - SSD_D_INNER - bc_width
            c_s[:, off:off + SSD_CONV_PIECE] = v.astype(BF16)

    raw = dtraw_ref[...] + dtb_ref[...]
    dt = jnp.maximum(raw, 0.0) + jnp.log1p(jnp.exp(-jnp.abs(raw)))
    adt = dt * (-jnp.exp(alog_ref[...]) * LOG2E)
    tril = tril_ref[...]
    acum = sum(_dot(tril, p) for p in _split3(adt))
    rsrc_t = (acum - jnp.log(dt) * LOG2E).T
    expand = expand_ref[...]
    ea_x = sum(_dot(p, expand) for p in _split2(jnp.exp2(acum)))
    wst = jnp.exp2(acum[L - 1:L, :] - acum) * dt
    wst_x = sum(_dot(p, expand) for p in _split2(wst))

    row = lax.broadcasted_iota(jnp.int32, (L, L), 0)
    col = lax.broadcasted_iota(jnp.int32, (L, L), 1)
    causal = col <= row
    low_half = lax.broadcasted_iota(jnp.int32, (L, LANES), 1) < SSD_HEAD_DIM
    for g in range(SSD_GROUPS):
        gl = slice(g * SSD_GROUP_WIDTH, (g + 1) * SSD_GROUP_WIDTH)
        nl = slice(g * SSD_STATE, (g + 1) * SSD_STATE)
        bg = b_s[:, nl]
        cg = c_s[:, nl]
        cb = _dot_nt(cg, bg)
        xg = xc_s[:, gl]
        halves = []
        for t in range(2):
            xt = xg[:, t * LANES:(t + 1) * LANES]
            acc = None
            for u in range(2):
                hd = g * SSD_HEADS_PER_GROUP + 2 * t + u
                seg = acum[:, hd:hd + 1] - rsrc_t[hd:hd + 1, :]
                m = (cb * jnp.where(causal, jnp.exp2(seg), 0.0)).astype(BF16)
                keep = low_half if u == 0 else jnp.logical_not(low_half)
                d = _dot(m, jnp.where(keep, xt, 0.0).astype(BF16))
                acc = d if acc is None else acc + d
            halves.append(acc)
        y = jnp.concatenate(halves, axis=1)

        st = state[g]
        y = y + _dot(cg, st.astype(BF16)) * ea_x[:, gl]
        state[g] = (st * ea_x[L - 1:L, gl]
                    + _dot(bg.astype(F32).T.astype(BF16), (xg * wst_x[:, gl]).astype(BF16)))

        y = y + dskip_ref[:, gl] * xg
        gated = y * _silu(z_ref[:, gl].astype(F32))
        ms = jnp.mean(gated * gated, axis=-1, keepdims=True)
        o_ref[:, gl] = (gated * lax.rsqrt(ms + NORM_EPS) * nw_ref[:, gl]).astype(BF16)


def _ssd(proj, dt_raw, conv_w, conv_b, dt_bias, a_log, d_skip, ssd_norm_w):
    def pad_heads(p):
        return jnp.pad(p.reshape(1, SSD_HEADS), ((0, 0), (0, LANES - SSD_HEADS)))

    expand = np.zeros((LANES, SSD_D_INNER), np.float32)
    for hd in range(SSD_HEADS):
        expand[hd, hd * SSD_HEAD_DIM:(hd + 1) * SSD_HEAD_DIM] = 1.0
    tril = np.tril(np.ones((SSD_L, SSD_L), np.float32))
    dskip_x = jnp.repeat(d_skip, SSD_HEAD_DIM).reshape(1, SSD_D_INNER)

    w = SSD_D_INNER
    xb, bcb, zb = COL_SSD_X // w, COL_SSD_B // w, COL_SSD_Z // w
    rows = lambda b, c: b * SSD_NC + c
    full = lambda b, c: (0, 0)
    return pl.pallas_call(
        _ssd_kernel,
        grid=(BATCH, SSD_NC),
        in_specs=[
            pl.BlockSpec((SSD_L, w), lambda b, c: (rows(b, c), xb)),
            pl.BlockSpec((SSD_L, w), lambda b, c: (rows(b, c), bcb)),
            pl.BlockSpec((SSD_L, w), lambda b, c: (rows(b, c), zb)),
            pl.BlockSpec((SSD_L, LANES), lambda b, c: (rows(b, c), 0)),
            pl.BlockSpec((SSD_CONV, SSD_XBC), full),
            pl.BlockSpec((1, SSD_XBC), full),
            pl.BlockSpec((1, LANES), full),
            pl.BlockSpec((1, LANES), full),
            pl.BlockSpec((1, w), full),
            pl.BlockSpec((1, w), full),
            pl.BlockSpec((SSD_L, SSD_L), full),
            pl.BlockSpec((LANES, w), full),
        ],
        out_specs=pl.BlockSpec((SSD_L, w), lambda b, c: (rows(b, c), 0)),
        out_shape=jax.ShapeDtypeStruct((TOKENS, SSD_D_INNER), BF16),
        scratch_shapes=[
            pltpu.VMEM((SSD_ROWS, SSD_XBC), F32),
            pltpu.VMEM((SSD_L, SSD_D_INNER), F32),
            pltpu.VMEM((SSD_L, SSD_GROUPS * SSD_STATE), BF16),
            pltpu.VMEM((SSD_L, SSD_GROUPS * SSD_STATE), BF16),
            pltpu.VMEM((SSD_GROUPS, SSD_STATE, SSD_GROUP_WIDTH), F32),
        ],
        compiler_params=pltpu.CompilerParams(
            dimension_semantics=("parallel", "arbitrary"), vmem_limit_bytes=VMEM_LIMIT),
        name="ssd",
    )(proj, proj, proj, dt_raw, conv_w, conv_b,
      pad_heads(dt_bias), pad_heads(a_log), dskip_x, ssd_norm_w.reshape(1, SSD_D_INNER),
      jnp.asarray(tril, dtype=BF16), jnp.asarray(expand, dtype=BF16))


MERGE_TM = 256


def _merge_out_kernel(x_ref, osb_ref, ossd_ref, omem_ref, gsb_ref, gssd_ref, gmem_ref, bg_ref,
                      wsb_ref, wssd_ref, wmem_ref, wout_ref, fnw_ref, o_ref):
    def gate(g_ref, k):
        return _sigmoid(g_ref[...].astype(F32) + bg_ref[:, k * D_MODEL:(k + 1) * D_MODEL])

    merged = gate(gsb_ref, 0) * _dot(osb_ref[...], wsb_ref[...])
    merged = merged + gate(gssd_ref, 1) * _dot(ossd_ref[...], wssd_ref[...])
    merged = merged + gate(gmem_ref, 2) * _dot(omem_ref[...], wmem_ref[...])
    y = x_ref[...] + _dot(merged.astype(BF16), wout_ref[...])
    ms = jnp.mean(y * y, axis=-1, keepdims=True)
    o_ref[...] = y * lax.rsqrt(ms + NORM_EPS) * fnw_ref[...]


def _merge_out(xt, o_sb, o_ssd, o_mem, proj, b_gate, w_sb, w_ssd, w_mem, w_out, final_norm_w):
    gb = COL_GATE // D_MODEL
    tile = lambda width: pl.BlockSpec((MERGE_TM, width), lambda i: (i, 0))
    resident = lambda shape: pl.BlockSpec(shape, lambda i: (0, 0), pipeline_mode=pl.Buffered(1))
    return pl.pallas_call(
        _merge_out_kernel,
        grid=(TOKENS // MERGE_TM,),
        in_specs=[
            tile(D_MODEL), tile(SB_WIDTH), tile(SSD_D_INNER), tile(MEM_WIDTH),
            pl.BlockSpec((MERGE_TM, D_MODEL), lambda i: (i, gb)),
            pl.BlockSpec((MERGE_TM, D_MODEL), lambda i: (i, gb + 1)),
            pl.BlockSpec((MERGE_TM, D_MODEL), lambda i: (i, gb + 2)),
            resident((1, 3 * D_MODEL)),
            resident((SB_WIDTH, D_MODEL)), resident((SSD_D_INNER, D_MODEL)),
            resident((MEM_WIDTH, D_MODEL)), resident((D_MODEL, D_MODEL)),
            resident((1, D_MODEL)),
        ],
        out_specs=tile(D_MODEL),
        out_shape=jax.ShapeDtypeStruct((TOKENS, D_MODEL), F32),
        compiler_params=pltpu.CompilerParams(
            dimension_semantics=("parallel",), vmem_limit_bytes=VMEM_LIMIT),
        name="merge_out",
    )(xt, o_sb, o_ssd, o_mem, proj, proj, proj, b_gate, w_sb, w_ssd, w_mem, w_out, final_norm_w)


MEMKV_TN = 512
MEM_TM = 512


def _mem_kv_kernel(m_ref, nw_ref, w_ref, o_ref):
    m = m_ref[...]
    ms = jnp.mean(m * m, axis=-1, keepdims=True)
    mn = (m * lax.rsqrt(ms + NORM_EPS) * nw_ref[...]).astype(BF16)
    o_ref[...] = _dot(mn, w_ref[...]).astype(BF16)


def _mem_kv(mem2, mem_norm_w, w_kv):
    rows = BATCH * MEM_LEN
    return pl.pallas_call(
        _mem_kv_kernel,
        grid=(2 * MEM_WIDTH // MEMKV_TN,),
        in_specs=[
            pl.BlockSpec((rows, D_MODEL), lambda j: (0, 0)),
            pl.BlockSpec((1, D_MODEL), lambda j: (0, 0)),
            pl.BlockSpec((D_MODEL, MEMKV_TN), lambda j: (0, j)),
        ],
        out_specs=pl.BlockSpec((rows, MEMKV_TN), lambda j: (0, j)),
        out_shape=jax.ShapeDtypeStruct((rows, 2 * MEM_WIDTH), BF16),
        compiler_params=pltpu.CompilerParams(
            dimension_semantics=("parallel",), vmem_limit_bytes=VMEM_LIMIT),
        name="mem_kv",
    )(mem2, mem_norm_w, w_kv)


def _mem_attn_kernel(q_ref, z_ref, kv_ref, o_ref):
    scale = MEM_HEAD_DIM ** -0.5
    for hd in range(MEM_HEADS):
        lo, hi = hd * MEM_HEAD_DIM, (hd + 1) * MEM_HEAD_DIM
        s = _dot_nt(q_ref[:, lo:hi], kv_ref[:, lo:hi]) * scale
        p = jnp.exp(s - jnp.max(s, axis=-1, keepdims=True))
        den = jnp.sum(p, axis=-1, keepdims=True)
        o = _dot(p.astype(BF16), kv_ref[:, MEM_WIDTH + lo:MEM_WIDTH + hi]) / den
        z = z_ref[:, lo:hi].astype(F32)
        o_ref[:, lo:hi] = (o * _silu(z)).astype(BF16)


def _mem_attn(proj, kv):
    nt = SEQ // MEM_TM
    qb, zb = COL_MEM_Q // MEM_WIDTH, COL_MEM_Z // MEM_WIDTH
    return pl.pallas_call(
        _mem_attn_kernel,
        grid=(BATCH, nt),
        in_specs=[
            pl.BlockSpec((MEM_TM, MEM_WIDTH), lambda b, i: (b * nt + i, qb)),
            pl.BlockSpec((MEM_TM, MEM_WIDTH), lambda b, i: (b * nt + i, zb)),
            pl.BlockSpec((MEM_LEN, 2 * MEM_WIDTH), lambda b, i: (b, 0)),
        ],
        out_specs=pl.BlockSpec((MEM_TM, MEM_WIDTH), lambda b, i: (b * nt + i, 0)),
        out_shape=jax.ShapeDtypeStruct((TOKENS, MEM_WIDTH), BF16),
        compiler_params=pltpu.CompilerParams(
            dimension_semantics=("parallel", "parallel"), vmem_limit_bytes=VMEM_LIMIT),
        name="mem_attn",
    )(proj, proj, kv)


def kernel(x, mem, norm_w, mem_norm_w, w_in, b_gate, conv_w, conv_b, dt_bias, a_log, d_skip,
           ssd_norm_w, w_mem_kv, w_branch_sb, w_branch_ssd, w_branch_mem, w_out, final_norm_w):
    xt = x.reshape(TOKENS, D_MODEL)
    w_in_t = w_in[0].T

    h, dt_raw = _norm_in(xt, norm_w[0].reshape(1, D_MODEL), w_in_t)
    proj, (w_kv, w_sb, w_ssd, w_mem, w_o) = _in_proj(
        h, w_in_t, [w_mem_kv[0], w_branch_sb[0], w_branch_ssd[0], w_branch_mem[0], w_out[0]])
    o_sb = _sb_attn(proj)
    o_ssd = _ssd(proj, dt_raw, conv_w[0], conv_b[0].reshape(1, -1), dt_bias[0], a_log[0],
                 d_skip[0], ssd_norm_w[0])
    kv = _mem_kv(mem.reshape(BATCH * MEM_LEN, D_MODEL), mem_norm_w[0].reshape(1, D_MODEL), w_kv)
    o_mem = _mem_attn(proj, kv)
    out = _merge_out(xt, o_sb, o_ssd, o_mem, proj, b_gate[0].reshape(1, -1),
                     w_sb, w_ssd, w_mem, w_o, final_norm_w.reshape(1, D_MODEL))
    return out.reshape(BATCH, SEQ, D_MODEL)
```

```python
import numpy as np
import jax
import jax.numpy as jnp
from jax import lax
from jax.experimental import pallas as pl
from jax.experimental.pallas import tpu as pltpu

F32 = jnp.float32
BF16 = jnp.bfloat16

D_MODEL = 2048
BATCH = 2
SEQ = 4096
TOKENS = BATCH * SEQ

SB_HEADS = 8
SB_HEAD_DIM = 128
SB_WIDTH = SB_HEADS * SB_HEAD_DIM

SSD_D_INNER = 2048
SSD_HEAD_DIM = 64
SSD_HEADS = 32
SSD_GROUPS = 8
SSD_HEADS_PER_GROUP = 4
SSD_STATE = 128
SSD_CONV = 4
SSD_CHUNK = 256
SSD_GROUP_WIDTH = SSD_HEADS_PER_GROUP * SSD_HEAD_DIM

MEM_LEN = 256
MEM_HEADS = 4
MEM_HEAD_DIM = 256
MEM_WIDTH = MEM_HEADS * MEM_HEAD_DIM

NORM_EPS = 1e-6

COL_SB_Q = 0
COL_SB_K = 1024
COL_SB_V = 2048
COL_SB_Z = 3072
COL_SSD_Z = 4096
COL_SSD_X = 6144
COL_SSD_B = 8192
COL_SSD_C = 9216
COL_DT = 10240
DT_WIDTH = 32
COL_MEM_Q = 10240
COL_MEM_Z = 11264
COL_GATE = 12288
PROJ_WIDTH = 18432

LANES = 128
VMEM_LIMIT = 56 * 1024 * 1024


def _sigmoid(v):
    return 0.5 + 0.5 * jnp.tanh(0.5 * v)


def _silu(v):
    h = 0.5 * v
    return h + h * jnp.tanh(h)


def _split2(v):
    hi = v.astype(BF16)
    lo = (v - hi.astype(F32)).astype(BF16)
    return hi, lo


def _split3(v):
    hi = v.astype(BF16)
    r = v - hi.astype(F32)
    mid = r.astype(BF16)
    lo = (r - mid.astype(F32)).astype(BF16)
    return hi, mid, lo


def _dot(a, b):
    return jnp.dot(a, b, preferred_element_type=F32)


def _dot_nt(a, b):
    return lax.dot_general(a, b, (((1,), (1,)), ((), ())), preferred_element_type=F32)


NORM_TM = 512
PROJ_TM = 2048
PROJ_TN = 1024


def _norm_in_kernel(x_ref, nw_ref, wdt_ref, wq_ref, h_ref, dt_ref, q_ref, wqb_ref):
    @pl.when(pl.program_id(0) == 0)
    def _():
        wqb_ref[...] = wq_ref[...].astype(BF16)

    x = x_ref[...]
    ms = jnp.mean(x * x, axis=-1, keepdims=True)
    hb = (x * lax.rsqrt(ms + NORM_EPS) * nw_ref[...]).astype(BF16)
    h_ref[...] = hb
    dt_ref[...] = _dot_nt(hb, wdt_ref[...].astype(BF16))
    q_ref[...] = (_dot_nt(hb, wqb_ref[...]) * SB_Q_PRESCALE).astype(BF16)


def _norm_in(xt, norm_w, w_in_t):
    assert SB_WIDTH == PROJ_TN and COL_SB_Q == 0
    return pl.pallas_call(
        _norm_in_kernel,
        grid=(TOKENS // NORM_TM,),
        in_specs=[
            pl.BlockSpec((NORM_TM, D_MODEL), lambda i: (i, 0)),
            pl.BlockSpec((1, D_MODEL), lambda i: (0, 0)),
            pl.BlockSpec((LANES, D_MODEL), lambda i: (COL_DT // LANES, 0)),
            pl.BlockSpec((PROJ_TN, D_MODEL), lambda i: (0, 0), pipeline_mode=pl.Buffered(1)),
        ],
        out_specs=[
            pl.BlockSpec((NORM_TM, D_MODEL), lambda i: (i, 0)),
            pl.BlockSpec((NORM_TM, LANES), lambda i: (i, 0)),
            pl.BlockSpec((NORM_TM, PROJ_TN), lambda i: (i, 0)),
        ],
        out_shape=[
            jax.ShapeDtypeStruct((TOKENS, D_MODEL), BF16),
            jax.ShapeDtypeStruct((TOKENS, LANES), F32),
            jax.ShapeDtypeStruct((TOKENS, PROJ_WIDTH), BF16),
        ],
        scratch_shapes=[pltpu.VMEM((PROJ_TN, D_MODEL), BF16)],
        compiler_params=pltpu.CompilerParams(
            dimension_semantics=("arbitrary",), vmem_limit_bytes=VMEM_LIMIT),
        name="norm_in",
    )(xt, norm_w, w_in_t, w_in_t)


PROJ_SHIFT_TILE = COL_DT // PROJ_TN


PROJ_CAST_SLICES = 64
PROJ_VMEM_LIMIT = 60 * 1024 * 1024


def _in_proj_kernel(h_ref, w_ref, wnext_ref, proj_in_ref, *rest):
    del proj_in_ref
    n_side = (len(rest) - 2) // 2
    side_in, o_ref, side_out, wb_ref = rest[:n_side], rest[n_side], rest[n_side + 1:-1], rest[-1]
    for src, dst in zip(side_in, side_out):
        dst[...] = src[...].astype(BF16)

    tile = pl.program_id(0) + 1

    @pl.when(pl.program_id(1) == 0)
    def _():
        @pl.when(tile < PROJ_SHIFT_TILE)
        def _():
            wb_ref[...] = w_ref[...].astype(BF16)

        @pl.when(tile >= PROJ_SHIFT_TILE)
        def _():
            keep = PROJ_TN - DT_WIDTH
            wb_ref[:keep, :] = w_ref[DT_WIDTH:, :].astype(BF16)
            wb_ref[keep:, :] = wnext_ref[...].astype(BF16)

    o_ref[...] = _dot_nt(h_ref[...], wb_ref[...]).astype(BF16)


def _in_proj(h, w_in_t, proj, side_weights):
    next_per_tile = PROJ_TN // DT_WIDTH
    steps_i = TOKENS // PROJ_TM
    grid = (PROJ_WIDTH // PROJ_TN - 1, steps_i)
    assert grid[0] * grid[1] >= PROJ_CAST_SLICES
    slice_map = lambda j, i: (jnp.minimum(j * steps_i + i, PROJ_CAST_SLICES - 1), 0)
    side_specs = [pl.BlockSpec((w.shape[0] // PROJ_CAST_SLICES, D_MODEL), slice_map)
                  for w in side_weights]
    outs = pl.pallas_call(
        _in_proj_kernel,
        grid=grid,
        in_specs=[
            pl.BlockSpec((PROJ_TM, D_MODEL), lambda j, i: (i, 0)),
            pl.BlockSpec((PROJ_TN, D_MODEL), lambda j, i: (j + 1, 0)),
            pl.BlockSpec((DT_WIDTH, D_MODEL), lambda j, i: ((j + 2) * next_per_tile, 0)),
            pl.BlockSpec(memory_space=pl.ANY),
        ] + side_specs,
        out_specs=[pl.BlockSpec((PROJ_TM, PROJ_TN), lambda j, i: (i, j + 1))] + side_specs,
        out_shape=[jax.ShapeDtypeStruct((TOKENS, PROJ_WIDTH), BF16)]
        + [jax.ShapeDtypeStruct(w.shape, BF16) for w in side_weights],
        scratch_shapes=[pltpu.VMEM((PROJ_TN, D_MODEL), BF16)],
        input_output_aliases={3: 0},
        compiler_params=pltpu.CompilerParams(
            dimension_semantics=("arbitrary", "arbitrary"), vmem_limit_bytes=PROJ_VMEM_LIMIT),
        name="in_proj",
    )(h, w_in_t, w_in_t, proj, *side_weights)
    return outs[0], outs[1:]


SB_TQ = 256
SB_TK = 256
SB_NQ = SEQ // SB_TQ
SB_HPS = 8
SB_STEP_WIDTH = SB_HPS * SB_HEAD_DIM
LOG2E = 1.4426950408889634
SB_Q_PRESCALE = -(SB_HEAD_DIM ** -0.5) * LOG2E
SB_DEAD_LOG2 = -160.0


def _sb_weights(ny, wcs, carry, mask):
    lse = jnp.log(1.0 + jnp.exp2(-jnp.abs(ny))) * LOG2E
    lk = jnp.minimum(ny, 0.0) - lse
    if mask is not None:
        lk = jnp.where(mask, lk, 0.0)
    hi, lo = _split2(lk)
    c1 = _dot(jnp.concatenate([hi[:, LANES:], lo[:, LANES:]], axis=1), wcs)
    c0 = _dot(jnp.concatenate([hi[:, :LANES], lo[:, :LANES]], axis=1), wcs)
    base0 = carry + c1[:, LANES:]
    zexp = jnp.concatenate([c0[:, :LANES] + base0, c1[:, :LANES] + carry], axis=1) - ny
    w = jnp.exp2(zexp)
    if mask is not None:
        w = jnp.where(mask, w, 0.0)
    return base0 + c0[:, LANES:], w.astype(BF16)


def _sb_attn_kernel(q_ref, k_ref, v_ref, z_ref, wcs_ref, o_ref, carry_ref, acc_ref, ny_ref, w_ref):
    i = pl.program_id(2)
    wcs = wcs_ref[...]
    row = lax.broadcasted_iota(jnp.int32, (SB_TQ, SB_TK), 0)
    col = lax.broadcasted_iota(jnp.int32, (SB_TQ, SB_TK), 1)
    heads = [slice(hh * SB_HEAD_DIM, (hh + 1) * SB_HEAD_DIM) for hh in range(SB_HPS)]

    def block_start(kb):
        return pl.multiple_of(kb * SB_TK, SB_TK)

    def qk(kb):
        for hh, sl in enumerate(heads):
            ny_ref[hh] = _dot_nt(q_ref[:, sl], k_ref[pl.ds(block_start(kb), SB_TK), sl])

    def pv(kb):
        for hh, sl in enumerate(heads):
            acc_ref[hh] += _dot(w_ref[hh], v_ref[pl.ds(block_start(kb), SB_TK), sl])

    def weights(mask):
        for hh in range(SB_HPS):
            carry_ref[hh], w_ref[hh] = _sb_weights(ny_ref[hh], wcs, carry_ref[hh], mask)

    def more_blocks(cur):
        top = carry_ref[0]
        for hh in range(1, SB_HPS):
            top = jnp.maximum(top, carry_ref[hh])
        return jnp.logical_and(cur > 0, jnp.max(top) > SB_DEAD_LOG2)

    carry_ref[...] = jnp.zeros(carry_ref.shape, F32)
    acc_ref[...] = jnp.zeros(acc_ref.shape, F32)
    qk(i)
    weights(col < row)
    qk(jnp.maximum(i - 1, 0))

    def body(state):
        last, _ = state
        cur = last - 1
        pv(last)
        weights(None)
        qk(jnp.maximum(cur - 1, 0))
        return cur, more_blocks(cur)

    last, _ = lax.while_loop(lambda state: state[1], body, (i, more_blocks(i)))
    pv(last)
    for hh, sl in enumerate(heads):
        z = z_ref[:, sl].astype(F32)
        o_ref[:, sl] = (acc_ref[hh] * _silu(z)).astype(BF16)


def _sb_cumsum_matrix():
    j = np.arange(2 * LANES)[:, None] % LANES
    s = np.arange(2 * LANES)[None, :]
    m = np.where(s < LANES, (j >= s), True)
    return jnp.asarray(m.astype(np.float32), dtype=BF16)


def _sb_attn(proj):
    w = SB_STEP_WIDTH
    qb, kb, vb, zb = COL_SB_Q // w, COL_SB_K // w, COL_SB_V // w, COL_SB_Z // w
    return pl.pallas_call(
        _sb_attn_kernel,
        grid=(BATCH, SB_HEADS // SB_HPS, SB_NQ),
        in_specs=[
            pl.BlockSpec((SB_TQ, w), lambda b, h, i: (b * SB_NQ + i, qb + h)),
            pl.BlockSpec((SEQ, w), lambda b, h, i: (b, kb + h)),
            pl.BlockSpec((SEQ, w), lambda b, h, i: (b, vb + h)),
            pl.BlockSpec((SB_TQ, w), lambda b, h, i: (b * SB_NQ + i, zb + h)),
            pl.BlockSpec((2 * LANES, 2 * LANES), lambda b, h, i: (0, 0)),
        ],
        out_specs=pl.BlockSpec((SB_TQ, w), lambda b, h, i: (b * SB_NQ + i, h)),
        out_shape=jax.ShapeDtypeStruct((TOKENS, SB_WIDTH), BF16),
        scratch_shapes=[
            pltpu.VMEM((SB_HPS, SB_TQ, LANES), F32),
            pltpu.VMEM((SB_HPS, SB_TQ, LANES), F32),
            pltpu.VMEM((SB_HPS, SB_TQ, SB_TK), F32),
            pltpu.VMEM((SB_HPS, SB_TQ, SB_TK), BF16),
        ],
        compiler_params=pltpu.CompilerParams(
            dimension_semantics=("parallel", "parallel", "arbitrary"),
            vmem_limit_bytes=VMEM_LIMIT),
        name="sb_attn",
    )(proj, proj, proj, proj, _sb_cumsum_matrix())


SSD_L = 128
SSD_NC = SEQ // SSD_L
SSD_PAD = 8
SSD_ROWS = SSD_PAD + SSD_L
SSD_XBC = SSD_D_INNER + 2 * SSD_GROUPS * SSD_STATE
SSD_CONV_PIECE = 512


def _ssd_kernel(x_ref, bc_ref, z_ref, dtraw_ref, cw_ref, cbias_ref, dtb_ref, alog_ref,
                dskip_ref, nw_ref, tril_ref, expand_ref,
                o_ref, pad, xc_s, b_s, c_s, state):
    c = pl.program_id(1)
    L = SSD_L

    @pl.when(c == 0)
    def _():
        pad[0:SSD_PAD, :] = jnp.zeros((SSD_PAD, SSD_XBC), F32)
        state[...] = jnp.zeros(state.shape, F32)

    @pl.when(c > 0)
    def _():
        pad[0:SSD_PAD, :] = pad[L:SSD_ROWS, :]

    pad[SSD_PAD:SSD_ROWS, 0:SSD_D_INNER] = x_ref[...].astype(F32)
    pad[SSD_PAD:SSD_ROWS, SSD_D_INNER:SSD_XBC] = bc_ref[...].astype(F32)

    bc_width = SSD_GROUPS * SSD_STATE
    for p in range(SSD_XBC // SSD_CONV_PIECE):
        lo = p * SSD_CONV_PIECE
        cols = slice(lo, lo + SSD_CONV_PIECE)
        wh = 0.5 * cw_ref[:, cols]
        u = pad[:, cols]
        u1 = pltpu.roll(u, 1, 0)
        older = u1 * wh[0:1, :] + u * wh[1:2, :]
        newer = u1 * wh[2:3, :] + u * wh[3:4, :]
        acc = pltpu.roll(older, 2, 0) + newer
        h = acc[SSD_PAD:SSD_ROWS, :] + 0.5 * cbias_ref[:, cols]
        v = h + h * jnp.tanh(h)
        if lo < SSD_D_INNER:
            xc_s[:, cols] = v
        elif lo < SSD_D_INNER + bc_width:
            b_s[:, lo - SSD_D_INNER:lo - SSD_D_INNER + SSD_CONV_PIECE] = v.astype(BF16)
        else:
            off = lo - SSD_D_INNER - bc_width
            c_s[:, off:off + SSD_CONV_PIECE] = v.astype(BF16)

    raw = dtraw_ref[...] + dtb_ref[...]
    dt = jnp.maximum(raw, 0.0) + jnp.log1p(jnp.exp(-jnp.abs(raw)))
    adt = dt * (-jnp.exp(alog_ref[...]) * LOG2E)
    tril = tril_ref[...]
    acum = sum(_dot(tril, p) for p in _split3(adt))
    rsrc_t = (acum - jnp.log(dt) * LOG2E).T
    expand = expand_ref[...]
    ea_x = sum(_dot(p, expand) for p in _split2(jnp.exp2(acum)))
    wst = jnp.exp2(acum[L - 1:L, :] - acum) * dt
    wst_x = sum(_dot(p, expand) for p in _split2(wst))

    row = lax.broadcasted_iota(jnp.int32, (L, L), 0)
    col = lax.broadcasted_iota(jnp.int32, (L, L), 1)
    causal = col <= row
    low_half = lax.broadcasted_iota(jnp.int32, (L, LANES), 1) < SSD_HEAD_DIM
    for g in range(SSD_GROUPS):
        gl = slice(g * SSD_GROUP_WIDTH, (g + 1) * SSD_GROUP_WIDTH)
        nl = slice(g * SSD_STATE, (g + 1) * SSD_STATE)
        bg = b_s[:, nl]
        cg = c_s[:, nl]
        cb = _dot_nt(cg, bg)
        xg = xc_s[:, gl]
        halves = []
        for t in range(2):
            xt = xg[:, t * LANES:(t + 1) * LANES]
            acc = None
            for u in range(2):
                hd = g * SSD_HEADS_PER_GROUP + 2 * t + u
                seg = acum[:, hd:hd + 1] - rsrc_t[hd:hd + 1, :]
                m = (cb * jnp.where(causal, jnp.exp2(seg), 0.0)).astype(BF16)
                keep = low_half if u == 0 else jnp.logical_not(low_half)
                d = _dot(m, jnp.where(keep, xt, 0.0).astype(BF16))
                acc = d if acc is None else acc + d
            halves.append(acc)
        y = jnp.concatenate(halves, axis=1)

        st = state[g]
        y = y + _dot(cg, st.astype(BF16)) * ea_x[:, gl]
        state[g] = (st * ea_x[L - 1:L, gl]
                    + _dot(bg.astype(F32).T.astype(BF16), (xg * wst_x[:, gl]).astype(BF16)))

        y = y + dskip_ref[:, gl] * xg
        gated = y * _silu(z_ref[:, gl].astype(F32))
        ms = jnp.mean(gated * gated, axis=-1, keepdims=True)
        o_ref[:, gl] = (gated * lax.rsqrt(ms + NORM_EPS) * nw_ref[:, gl]).astype(BF16)


def _ssd(proj, dt_raw, conv_w, conv_b, dt_bias, a_log, d_skip, ssd_norm_w):
    def pad_heads(p):
        return jnp.pad(p.reshape(1, SSD_HEADS), ((0, 0), (0, LANES - SSD_HEADS)))

    expand = np.zeros((LANES, SSD_D_INNER), np.float32)
    for hd in range(SSD_HEADS):
        expand[hd, hd * SSD_HEAD_DIM:(hd + 1) * SSD_HEAD_DIM] = 1.0
    tril = np.tril(np.ones((SSD_L, SSD_L), np.float32))
    dskip_x = jnp.repeat(d_skip, SSD_HEAD_DIM).reshape(1, SSD_D_INNER)

    w = SSD_D_INNER
    xb, bcb, zb = COL_SSD_X // w, COL_SSD_B // w, COL_SSD_Z // w
    rows = lambda b, c: b * SSD_NC + c
    full = lambda b, c: (0, 0)
    return pl.pallas_call(
        _ssd_kernel,
        grid=(BATCH, SSD_NC),
        in_specs=[
            pl.BlockSpec((SSD_L, w), lambda b, c: (rows(b, c), xb)),
            pl.BlockSpec((SSD_L, w), lambda b, c: (rows(b, c), bcb)),
            pl.BlockSpec((SSD_L, w), lambda b, c: (rows(b, c), zb)),
            pl.BlockSpec((SSD_L, LANES), lambda b, c: (rows(b, c), 0)),
            pl.BlockSpec((SSD_CONV, SSD_XBC), full),
            pl.BlockSpec((1, SSD_XBC), full),
            pl.BlockSpec((1, LANES), full),
            pl.BlockSpec((1, LANES), full),
            pl.BlockSpec((1, w), full),
            pl.BlockSpec((1, w), full),
            pl.BlockSpec((SSD_L, SSD_L), full),
            pl.BlockSpec((LANES, w), full),
        ],
        out_specs=pl.BlockSpec((SSD_L, w), lambda b, c: (rows(b, c), 0)),
        out_shape=jax.ShapeDtypeStruct((TOKENS, SSD_D_INNER), BF16),
        scratch_shapes=[
            pltpu.VMEM((SSD_ROWS, SSD_XBC), F32),
            pltpu.VMEM((SSD_L, SSD_D_INNER), F32),
            pltpu.VMEM((SSD_L, SSD_GROUPS * SSD_STATE), BF16),
            pltpu.VMEM((SSD_L, SSD_GROUPS * SSD_STATE), BF16),
            pltpu.VMEM((SSD_GROUPS, SSD_STATE, SSD_GROUP_WIDTH), F32),
        ],
        compiler_params=pltpu.CompilerParams(
            dimension_semantics=("parallel", "arbitrary"), vmem_limit_bytes=VMEM_LIMIT),
        name="ssd",
    )(proj, proj, proj, dt_raw, conv_w, conv_b,
      pad_heads(dt_bias), pad_heads(a_log), dskip_x, ssd_norm_w.reshape(1, SSD_D_INNER),
      jnp.asarray(tril, dtype=BF16), jnp.asarray(expand, dtype=BF16))


MERGE_TM = 256


def _merge_out_kernel(x_ref, osb_ref, ossd_ref, omem_ref, gsb_ref, gssd_ref, gmem_ref, bg_ref,
                      wsb_ref, wssd_ref, wmem_ref, wout_ref, fnw_ref, o_ref):
    def gate(g_ref, k):
        return _sigmoid(g_ref[...].astype(F32) + bg_ref[:, k * D_MODEL:(k + 1) * D_MODEL])

    merged = gate(gsb_ref, 0) * _dot(osb_ref[...], wsb_ref[...])
    merged = merged + gate(gssd_ref, 1) * _dot(ossd_ref[...], wssd_ref[...])
    merged = merged + gate(gmem_ref, 2) * _dot(omem_ref[...], wmem_ref[...])
    y = x_ref[...] + _dot(merged.astype(BF16), wout_ref[...])
    ms = jnp.mean(y * y, axis=-1, keepdims=True)
    o_ref[...] = y * lax.rsqrt(ms + NORM_EPS) * fnw_ref[...]


def _merge_out(xt, o_sb, o_ssd, o_mem, proj, b_gate, w_sb, w_ssd, w_mem, w_out, final_norm_w):
    gb = COL_GATE // D_MODEL
    tile = lambda width: pl.BlockSpec((MERGE_TM, width), lambda i: (i, 0))
    resident = lambda shape: pl.BlockSpec(shape, lambda i: (0, 0), pipeline_mode=pl.Buffered(1))
    return pl.pallas_call(
        _merge_out_kernel,
        grid=(TOKENS // MERGE_TM,),
        in_specs=[
            tile(D_MODEL), tile(SB_WIDTH), tile(SSD_D_INNER), tile(MEM_WIDTH),
            pl.BlockSpec((MERGE_TM, D_MODEL), lambda i: (i, gb)),
            pl.BlockSpec((MERGE_TM, D_MODEL), lambda i: (i, gb + 1)),
            pl.BlockSpec((MERGE_TM, D_MODEL), lambda i: (i, gb + 2)),
            resident((1, 3 * D_MODEL)),
            resident((SB_WIDTH, D_MODEL)), resident((SSD_D_INNER, D_MODEL)),
            resident((MEM_WIDTH, D_MODEL)), resident((D_MODEL, D_MODEL)),
            resident((1, D_MODEL)),
        ],
        out_specs=tile(D_MODEL),
        out_shape=jax.ShapeDtypeStruct((TOKENS, D_MODEL), F32),
        compiler_params=pltpu.CompilerParams(
            dimension_semantics=("parallel",), vmem_limit_bytes=VMEM_LIMIT),
        name="merge_out",
    )(xt, o_sb, o_ssd, o_mem, proj, proj, proj, b_gate, w_sb, w_ssd, w_mem, w_out, final_norm_w)


MEMKV_TN = 512
MEM_TM = 512


def _mem_kv_kernel(m_ref, nw_ref, w_ref, o_ref):
    m = m_ref[...]
    ms = jnp.mean(m * m, axis=-1, keepdims=True)
    mn = (m * lax.rsqrt(ms + NORM_EPS) * nw_ref[...]).astype(BF16)
    o_ref[...] = _dot(mn, w_ref[...]).astype(BF16)


def _mem_kv(mem2, mem_norm_w, w_kv):
    rows = BATCH * MEM_LEN
    return pl.pallas_call(
        _mem_kv_kernel,
        grid=(2 * MEM_WIDTH // MEMKV_TN,),
        in_specs=[
            pl.BlockSpec((rows, D_MODEL), lambda j: (0, 0)),
            pl.BlockSpec((1, D_MODEL), lambda j: (0, 0)),
            pl.BlockSpec((D_MODEL, MEMKV_TN), lambda j: (0, j)),
        ],
        out_specs=pl.BlockSpec((rows, MEMKV_TN), lambda j: (0, j)),
        out_shape=jax.ShapeDtypeStruct((rows, 2 * MEM_WIDTH), BF16),
        compiler_params=pltpu.CompilerParams(
            dimension_semantics=("parallel",), vmem_limit_bytes=VMEM_LIMIT),
        name="mem_kv",
    )(mem2, mem_norm_w, w_kv)


def _mem_attn_kernel(q_ref, z_ref, kv_ref, o_ref):
    scale = MEM_HEAD_DIM ** -0.5
    for hd in range(MEM_HEADS):
        lo, hi = hd * MEM_HEAD_DIM, (hd + 1) * MEM_HEAD_DIM
        s = _dot_nt(q_ref[:, lo:hi], kv_ref[:, lo:hi]) * scale
        p = jnp.exp(s - jnp.max(s, axis=-1, keepdims=True))
        den = jnp.sum(p, axis=-1, keepdims=True)
        o = _dot(p.astype(BF16), kv_ref[:, MEM_WIDTH + lo:MEM_WIDTH + hi]) / den
        z = z_ref[:, lo:hi].astype(F32)
        o_ref[:, lo:hi] = (o * _silu(z)).astype(BF16)


def _mem_attn(proj, kv):
    nt = SEQ // MEM_TM
    qb, zb = COL_MEM_Q // MEM_WIDTH, COL_MEM_Z // MEM_WIDTH
    return pl.pallas_call(
        _mem_attn_kernel,
        grid=(BATCH, nt),
        in_specs=[
            pl.BlockSpec((MEM_TM, MEM_WIDTH), lambda b, i: (b * nt + i, qb)),
            pl.BlockSpec((MEM_TM, MEM_WIDTH), lambda b, i: (b * nt + i, zb)),
            pl.BlockSpec((MEM_LEN, 2 * MEM_WIDTH), lambda b, i: (b, 0)),
        ],
        out_specs=pl.BlockSpec((MEM_TM, MEM_WIDTH), lambda b, i: (b * nt + i, 0)),
        out_shape=jax.ShapeDtypeStruct((TOKENS, MEM_WIDTH), BF16),
        compiler_params=pltpu.CompilerParams(
            dimension_semantics=("parallel", "parallel"), vmem_limit_bytes=VMEM_LIMIT),
        name="mem_attn",
    )(proj, proj, kv)


def kernel(x, mem, norm_w, mem_norm_w, w_in, b_gate, conv_w, conv_b, dt_bias, a_log, d_skip,
           ssd_norm_w, w_mem_kv, w_branch_sb, w_branch_ssd, w_branch_mem, w_out, final_norm_w):
    xt = x.reshape(TOKENS, D_MODEL)
    w_in_t = w_in[0].T

    h, dt_raw, proj = _norm_in(xt, norm_w[0].reshape(1, D_MODEL), w_in_t)
    proj, (w_kv, w_sb, w_ssd, w_mem, w_o) = _in_proj(
        h, w_in_t, proj, [w_mem_kv[0], w_branch_sb[0], w_branch_ssd[0], w_branch_mem[0], w_out[0]])
    o_sb = _sb_attn(proj)
    o_ssd = _ssd(proj, dt_raw, conv_w[0], conv_b[0].reshape(1, -1), dt_bias[0], a_log[0],
                 d_skip[0], ssd_norm_w[0])
    kv = _mem_kv(mem.reshape(BATCH * MEM_LEN, D_MODEL), mem_norm_w[0].reshape(1, D_MODEL), w_kv)
    o_mem = _mem_attn(proj, kv)
    out = _merge_out(xt, o_sb, o_ssd, o_mem, proj, b_gate[0].reshape(1, -1),
                     w_sb, w_ssd, w_mem, w_o, final_norm_w.reshape(1, D_MODEL))
    return out.reshape(BATCH, SEQ, D_MODEL)
```

```python
import numpy as np
import jax
import jax.numpy as jnp
from jax import lax
from jax.experimental import pallas as pl
from jax.experimental.pallas import tpu as pltpu

F32 = jnp.float32
BF16 = jnp.bfloat16

D_MODEL = 2048
BATCH = 2
SEQ = 4096
TOKENS = BATCH * SEQ

SB_HEADS = 8
SB_HEAD_DIM = 128
SB_WIDTH = SB_HEADS * SB_HEAD_DIM

SSD_D_INNER = 2048
SSD_HEAD_DIM = 64
SSD_HEADS = 32
SSD_GROUPS = 8
SSD_HEADS_PER_GROUP = 4
SSD_STATE = 128
SSD_CONV = 4
SSD_CHUNK = 256
SSD_GROUP_WIDTH = SSD_HEADS_PER_GROUP * SSD_HEAD_DIM

MEM_LEN = 256
MEM_HEADS = 4
MEM_HEAD_DIM = 256
MEM_WIDTH = MEM_HEADS * MEM_HEAD_DIM

NORM_EPS = 1e-6

COL_SB_Q = 0
COL_SB_K = 1024
COL_SB_V = 2048
COL_SB_Z = 3072
COL_SSD_Z = 4096
COL_SSD_X = 6144
COL_SSD_B = 8192
COL_SSD_C = 9216
COL_DT = 10240
DT_WIDTH = 32
COL_MEM_Q = 10240
COL_MEM_Z = 11264
COL_GATE = 12288
PROJ_WIDTH = 18432

LANES = 128
VMEM_LIMIT = 56 * 1024 * 1024


def _sigmoid(v):
    return 0.5 + 0.5 * jnp.tanh(0.5 * v)


def _silu(v):
    h = 0.5 * v
    return h + h * jnp.tanh(h)


def _split2(v):
    hi = v.astype(BF16)
    lo = (v - hi.astype(F32)).astype(BF16)
    return hi, lo


def _split3(v):
    hi = v.astype(BF16)
    r = v - hi.astype(F32)
    mid = r.astype(BF16)
    lo = (r - mid.astype(F32)).astype(BF16)
    return hi, mid, lo


def _dot(a, b):
    return jnp.dot(a, b, preferred_element_type=F32)


def _dot_nt(a, b):
    return lax.dot_general(a, b, (((1,), (1,)), ((), ())), preferred_element_type=F32)


NORM_TM = 1024
PROJ_TM = 2048
PROJ_TN = 1024


def _norm_in_kernel(x_ref, nw_ref, wdt_ref, wq_ref, h_ref, dt_ref, q_ref, wqb_ref):
    @pl.when(pl.program_id(0) == 0)
    def _():
        wqb_ref[...] = wq_ref[...].astype(BF16)

    x = x_ref[...]
    ms = jnp.mean(x * x, axis=-1, keepdims=True)
    hb = (x * lax.rsqrt(ms + NORM_EPS) * nw_ref[...]).astype(BF16)
    h_ref[...] = hb
    dt_ref[...] = _dot_nt(hb, wdt_ref[...].astype(BF16))
    q_ref[...] = (_dot_nt(hb, wqb_ref[...]) * SB_Q_PRESCALE).astype(BF16)


def _norm_in(xt, norm_w, w_in_t):
    assert SB_WIDTH == PROJ_TN and COL_SB_Q == 0
    return pl.pallas_call(
        _norm_in_kernel,
        grid=(TOKENS // NORM_TM,),
        in_specs=[
            pl.BlockSpec((NORM_TM, D_MODEL), lambda i: (i, 0)),
            pl.BlockSpec((1, D_MODEL), lambda i: (0, 0)),
            pl.BlockSpec((LANES, D_MODEL), lambda i: (COL_DT // LANES, 0)),
            pl.BlockSpec((PROJ_TN, D_MODEL), lambda i: (0, 0), pipeline_mode=pl.Buffered(1)),
        ],
        out_specs=[
            pl.BlockSpec((NORM_TM, D_MODEL), lambda i: (i, 0)),
            pl.BlockSpec((NORM_TM, LANES), lambda i: (i, 0)),
            pl.BlockSpec((NORM_TM, PROJ_TN), lambda i: (i, 0)),
        ],
        out_shape=[
            jax.ShapeDtypeStruct((TOKENS, D_MODEL), BF16),
            jax.ShapeDtypeStruct((TOKENS, LANES), F32),
            jax.ShapeDtypeStruct((TOKENS, PROJ_WIDTH), BF16),
        ],
        scratch_shapes=[pltpu.VMEM((PROJ_TN, D_MODEL), BF16)],
        compiler_params=pltpu.CompilerParams(
            dimension_semantics=("arbitrary",), vmem_limit_bytes=VMEM_LIMIT),
        name="norm_in",
    )(xt, norm_w, w_in_t, w_in_t)


PROJ_SHIFT_TILE = COL_DT // PROJ_TN


PROJ_CAST_SLICES = 64
PROJ_VMEM_LIMIT = 60 * 1024 * 1024


def _in_proj_kernel(h_ref, w_ref, wnext_ref, proj_in_ref, *rest):
    del proj_in_ref
    n_side = (len(rest) - 2) // 2
    side_in, o_ref, side_out, wb_ref = rest[:n_side], rest[n_side], rest[n_side + 1:-1], rest[-1]
    for src, dst in zip(side_in, side_out):
        dst[...] = src[...].astype(BF16)

    tile = pl.program_id(0) + 1

    @pl.when(pl.program_id(1) == 0)
    def _():
        @pl.when(tile < PROJ_SHIFT_TILE)
        def _():
            wb_ref[...] = w_ref[...].astype(BF16)

        @pl.when(tile >= PROJ_SHIFT_TILE)
        def _():
            keep = PROJ_TN - DT_WIDTH
            wb_ref[:keep, :] = w_ref[DT_WIDTH:, :].astype(BF16)
            wb_ref[keep:, :] = wnext_ref[...].astype(BF16)

    o_ref[...] = _dot_nt(h_ref[...], wb_ref[...]).astype(BF16)


def _in_proj(h, w_in_t, proj, side_weights):
    next_per_tile = PROJ_TN // DT_WIDTH
    steps_i = TOKENS // PROJ_TM
    grid = (PROJ_WIDTH // PROJ_TN - 1, steps_i)
    assert grid[0] * grid[1] >= PROJ_CAST_SLICES
    slice_map = lambda j, i: (jnp.minimum(j * steps_i + i, PROJ_CAST_SLICES - 1), 0)
    side_specs = [pl.BlockSpec((w.shape[0] // PROJ_CAST_SLICES, D_MODEL), slice_map)
                  for w in side_weights]
    outs = pl.pallas_call(
        _in_proj_kernel,
        grid=grid,
        in_specs=[
            pl.BlockSpec((PROJ_TM, D_MODEL), lambda j, i: (i, 0)),
            pl.BlockSpec((PROJ_TN, D_MODEL), lambda j, i: (j + 1, 0)),
            pl.BlockSpec((DT_WIDTH, D_MODEL), lambda j, i: ((j + 2) * next_per_tile, 0)),
            pl.BlockSpec(memory_space=pl.ANY),
        ] + side_specs,
        out_specs=[pl.BlockSpec((PROJ_TM, PROJ_TN), lambda j, i: (i, j + 1))] + side_specs,
        out_shape=[jax.ShapeDtypeStruct((TOKENS, PROJ_WIDTH), BF16)]
        + [jax.ShapeDtypeStruct(w.shape, BF16) for w in side_weights],
        scratch_shapes=[pltpu.VMEM((PROJ_TN, D_MODEL), BF16)],
        input_output_aliases={3: 0},
        compiler_params=pltpu.CompilerParams(
            dimension_semantics=("arbitrary", "arbitrary"), vmem_limit_bytes=PROJ_VMEM_LIMIT),
        name="in_proj",
    )(h, w_in_t, w_in_t, proj, *side_weights)
    return outs[0], outs[1:]


SB_TQ = 256
SB_TK = 256
SB_NQ = SEQ // SB_TQ
SB_HPS = 8
SB_STEP_WIDTH = SB_HPS * SB_HEAD_DIM
LOG2E = 1.4426950408889634
SB_Q_PRESCALE = -(SB_HEAD_DIM ** -0.5) * LOG2E
SB_DEAD_LOG2 = -160.0


def _sb_weights(ny, wcs, carry, mask):
    lse = jnp.log(1.0 + jnp.exp2(-jnp.abs(ny))) * LOG2E
    lk = jnp.minimum(ny, 0.0) - lse
    if mask is not None:
        lk = jnp.where(mask, lk, 0.0)
    hi, lo = _split2(lk)
    c1 = _dot(jnp.concatenate([hi[:, LANES:], lo[:, LANES:]], axis=1), wcs)
    c0 = _dot(jnp.concatenate([hi[:, :LANES], lo[:, :LANES]], axis=1), wcs)
    base0 = carry + c1[:, LANES:]
    zexp = jnp.concatenate([c0[:, :LANES] + base0, c1[:, :LANES] + carry], axis=1) - ny
    w = jnp.exp2(zexp)
    if mask is not None:
        w = jnp.where(mask, w, 0.0)
    return base0 + c0[:, LANES:], w.astype(BF16)


def _sb_attn_kernel(q_ref, k_ref, v_ref, z_ref, wcs_ref, o_ref, carry_ref, acc_ref, ny_ref, w_ref):
    i = pl.program_id(2)
    wcs = wcs_ref[...]
    row = lax.broadcasted_iota(jnp.int32, (SB_TQ, SB_TK), 0)
    col = lax.broadcasted_iota(jnp.int32, (SB_TQ, SB_TK), 1)
    heads = [slice(hh * SB_HEAD_DIM, (hh + 1) * SB_HEAD_DIM) for hh in range(SB_HPS)]

    def block_start(kb):
        return pl.multiple_of(kb * SB_TK, SB_TK)

    def qk(kb):
        for hh, sl in enumerate(heads):
            ny_ref[hh] = _dot_nt(q_ref[:, sl], k_ref[pl.ds(block_start(kb), SB_TK), sl])

    def pv(kb):
        for hh, sl in enumerate(heads):
            acc_ref[hh] += _dot(w_ref[hh], v_ref[pl.ds(block_start(kb), SB_TK), sl])

    def weights(mask):
        for hh in range(SB_HPS):
            carry_ref[hh], w_ref[hh] = _sb_weights(ny_ref[hh], wcs, carry_ref[hh], mask)

    def more_blocks(cur):
        top = carry_ref[0]
        for hh in range(1, SB_HPS):
            top = jnp.maximum(top, carry_ref[hh])
        return jnp.logical_and(cur > 0, jnp.max(top) > SB_DEAD_LOG2)

    carry_ref[...] = jnp.zeros(carry_ref.shape, F32)
    acc_ref[...] = jnp.zeros(acc_ref.shape, F32)
    qk(i)
    weights(col < row)
    qk(jnp.maximum(i - 1, 0))

    def body(state):
        last, _ = state
        cur = last - 1
        pv(last)
        weights(None)
        qk(jnp.maximum(cur - 1, 0))
        return cur, more_blocks(cur)

    last, _ = lax.while_loop(lambda state: state[1], body, (i, more_blocks(i)))
    pv(last)
    for hh, sl in enumerate(heads):
        z = z_ref[:, sl].astype(F32)
        o_ref[:, sl] = (acc_ref[hh] * _silu(z)).astype(BF16)


def _sb_cumsum_matrix():
    j = np.arange(2 * LANES)[:, None] % LANES
    s = np.arange(2 * LANES)[None, :]
    m = np.where(s < LANES, (j >= s), True)
    return jnp.asarray(m.astype(np.float32), dtype=BF16)


def _sb_attn(proj):
    w = SB_STEP_WIDTH
    qb, kb, vb, zb = COL_SB_Q // w, COL_SB_K // w, COL_SB_V // w, COL_SB_Z // w
    return pl.pallas_call(
        _sb_attn_kernel,
        grid=(BATCH, SB_HEADS // SB_HPS, SB_NQ),
        in_specs=[
            pl.BlockSpec((SB_TQ, w), lambda b, h, i: (b * SB_NQ + i, qb + h)),
            pl.BlockSpec((SEQ, w), lambda b, h, i: (b, kb + h)),
            pl.BlockSpec((SEQ, w), lambda b, h, i: (b, vb + h)),
            pl.BlockSpec((SB_TQ, w), lambda b, h, i: (b * SB_NQ + i, zb + h)),
            pl.BlockSpec((2 * LANES, 2 * LANES), lambda b, h, i: (0, 0)),
        ],
        out_specs=pl.BlockSpec((SB_TQ, w), lambda b, h, i: (b * SB_NQ + i, h)),
        out_shape=jax.ShapeDtypeStruct((TOKENS, SB_WIDTH), BF16),
        scratch_shapes=[
            pltpu.VMEM((SB_HPS, SB_TQ, LANES), F32),
            pltpu.VMEM((SB_HPS, SB_TQ, LANES), F32),
            pltpu.VMEM((SB_HPS, SB_TQ, SB_TK), F32),
            pltpu.VMEM((SB_HPS, SB_TQ, SB_TK), BF16),
        ],
        compiler_params=pltpu.CompilerParams(
            dimension_semantics=("parallel", "parallel", "arbitrary"),
            vmem_limit_bytes=VMEM_LIMIT),
        name="sb_attn",
    )(proj, proj, proj, proj, _sb_cumsum_matrix())


SSD_L = 128
SSD_STEP_ROWS = 256
SSD_NC = SEQ // SSD_STEP_ROWS
SSD_PAD = 8
SSD_ROWS = SSD_PAD + SSD_STEP_ROWS
SSD_XBC = SSD_D_INNER + 2 * SSD_GROUPS * SSD_STATE
SSD_CONV_PIECE = 512


def _ssd_kernel(x_ref, bc_ref, z_ref, dtraw_ref, cw_ref, cbias_ref, dtb_ref, alog_ref,
                dskip_ref, nw_ref, tril_ref, expand_ref,
                o_ref, pad, xc_s, b_s, c_s, state):
    c = pl.program_id(1)
    L = SSD_L

    @pl.when(c == 0)
    def _():
        pad[0:SSD_PAD, :] = jnp.zeros((SSD_PAD, SSD_XBC), F32)
        state[...] = jnp.zeros(state.shape, F32)

    @pl.when(c > 0)
    def _():
        pad[0:SSD_PAD, :] = pad[SSD_STEP_ROWS:SSD_ROWS, :]

    pad[SSD_PAD:SSD_ROWS, 0:SSD_D_INNER] = x_ref[...].astype(F32)
    pad[SSD_PAD:SSD_ROWS, SSD_D_INNER:SSD_XBC] = bc_ref[...].astype(F32)

    for sub in range(SSD_STEP_ROWS // L):
        _ssd_scan_chunk(sub * L, z_ref, dtraw_ref, cw_ref, cbias_ref, dtb_ref, alog_ref,
                        dskip_ref, nw_ref, tril_ref, expand_ref, o_ref, pad, xc_s, b_s, c_s, state)


def _ssd_scan_chunk(r0, z_ref, dtraw_ref, cw_ref, cbias_ref, dtb_ref, alog_ref,
                    dskip_ref, nw_ref, tril_ref, expand_ref, o_ref, pad, xc_s, b_s, c_s, state):
    L = SSD_L
    rows = slice(r0, r0 + L)

    bc_width = SSD_GROUPS * SSD_STATE
    for p in range(SSD_XBC // SSD_CONV_PIECE):
        lo = p * SSD_CONV_PIECE
        cols = slice(lo, lo + SSD_CONV_PIECE)
        wh = 0.5 * cw_ref[:, cols]
        u = pad[r0:r0 + SSD_PAD + L, cols]
        u1 = pltpu.roll(u, 1, 0)
        older = u1 * wh[0:1, :] + u * wh[1:2, :]
        newer = u1 * wh[2:3, :] + u * wh[3:4, :]
        acc = pltpu.roll(older, 2, 0) + newer
        h = acc[SSD_PAD:SSD_PAD + L, :] + 0.5 * cbias_ref[:, cols]
        v = h + h * jnp.tanh(h)
        if lo < SSD_D_INNER:
            xc_s[:, cols] = v
        elif lo < SSD_D_INNER + bc_width:
            b_s[:, lo - SSD_D_INNER:lo - SSD_D_INNER + SSD_CONV_PIECE] = v.astype(BF16)
        else:
            off = lo - SSD_D_INNER - bc_width
            c_s[:, off:off + SSD_CONV_PIECE] = v.astype(BF16)

    raw = dtraw_ref[rows, :] + dtb_ref[...]
    dt = jnp.maximum(raw, 0.0) + jnp.log1p(jnp.exp(-jnp.abs(raw)))
    adt = dt * (-jnp.exp(alog_ref[...]) * LOG2E)
    tril = tril_ref[...]
    acum = sum(_dot(tril, p) for p in _split3(adt))
    rsrc_t = (acum - jnp.log(dt) * LOG2E).T
    expand = expand_ref[...]
    ea_x = sum(_dot(p, expand) for p in _split2(jnp.exp2(acum)))
    wst = jnp.exp2(acum[L - 1:L, :] - acum) * dt
    wst_x = sum(_dot(p, expand) for p in _split2(wst))

    row = lax.broadcasted_iota(jnp.int32, (L, L), 0)
    col = lax.broadcasted_iota(jnp.int32, (L, L), 1)
    causal = col <= row
    low_half = lax.broadcasted_iota(jnp.int32, (L, LANES), 1) < SSD_HEAD_DIM
    for g in range(SSD_GROUPS):
        gl = slice(g * SSD_GROUP_WIDTH, (g + 1) * SSD_GROUP_WIDTH)
        nl = slice(g * SSD_STATE, (g + 1) * SSD_STATE)
        bg = b_s[:, nl]
        cg = c_s[:, nl]
        cb = _dot_nt(cg, bg)
        xg = xc_s[:, gl]
        halves = []
        for t in range(2):
            xt = xg[:, t * LANES:(t + 1) * LANES]
            acc = None
            for u in range(2):
                hd = g * SSD_HEADS_PER_GROUP + 2 * t + u
                seg = acum[:, hd:hd + 1] - rsrc_t[hd:hd + 1, :]
                m = (cb * jnp.where(causal, jnp.exp2(seg), 0.0)).astype(BF16)
                keep = low_half if u == 0 else jnp.logical_not(low_half)
                d = _dot(m, jnp.where(keep, xt, 0.0).astype(BF16))
                acc = d if acc is None else acc + d
            halves.append(acc)
        y = jnp.concatenate(halves, axis=1)

        st = state[g]
        y = y + _dot(cg, st.astype(BF16)) * ea_x[:, gl]
        state[g] = (st * ea_x[L - 1:L, gl]
                    + _dot(bg.astype(F32).T.astype(BF16), (xg * wst_x[:, gl]).astype(BF16)))

        y = y + dskip_ref[:, gl] * xg
        gated = y * _silu(z_ref[rows, gl].astype(F32))
        ms = jnp.mean(gated * gated, axis=-1, keepdims=True)
        o_ref[rows, gl] = (gated * lax.rsqrt(ms + NORM_EPS) * nw_ref[:, gl]).astype(BF16)


def _ssd(proj, dt_raw, conv_w, conv_b, dt_bias, a_log, d_skip, ssd_norm_w):
    def pad_heads(p):
        return jnp.pad(p.reshape(1, SSD_HEADS), ((0, 0), (0, LANES - SSD_HEADS)))

    expand = np.zeros((LANES, SSD_D_INNER), np.float32)
    for hd in range(SSD_HEADS):
        expand[hd, hd * SSD_HEAD_DIM:(hd + 1) * SSD_HEAD_DIM] = 1.0
    tril = np.tril(np.ones((SSD_L, SSD_L), np.float32))
    dskip_x = jnp.repeat(d_skip, SSD_HEAD_DIM).reshape(1, SSD_D_INNER)

    w = SSD_D_INNER
    xb, bcb, zb = COL_SSD_X // w, COL_SSD_B // w, COL_SSD_Z // w
    rows = lambda b, c: b * SSD_NC + c
    full = lambda b, c: (0, 0)
    return pl.pallas_call(
        _ssd_kernel,
        grid=(BATCH, SSD_NC),
        in_specs=[
            pl.BlockSpec((SSD_STEP_ROWS, w), lambda b, c: (rows(b, c), xb)),
            pl.BlockSpec((SSD_STEP_ROWS, w), lambda b, c: (rows(b, c), bcb)),
            pl.BlockSpec((SSD_STEP_ROWS, w), lambda b, c: (rows(b, c), zb)),
            pl.BlockSpec((SSD_STEP_ROWS, LANES), lambda b, c: (rows(b, c), 0)),
            pl.BlockSpec((SSD_CONV, SSD_XBC), full),
            pl.BlockSpec((1, SSD_XBC), full),
            pl.BlockSpec((1, LANES), full),
            pl.BlockSpec((1, LANES), full),
            pl.BlockSpec((1, w), full),
            pl.BlockSpec((1, w), full),
            pl.BlockSpec((SSD_L, SSD_L), full),
            pl.BlockSpec((LANES, w), full),
        ],
        out_specs=pl.BlockSpec((SSD_STEP_ROWS, w), lambda b, c: (rows(b, c), 0)),
        out_shape=jax.ShapeDtypeStruct((TOKENS, SSD_D_INNER), BF16),
        scratch_shapes=[
            pltpu.VMEM((SSD_ROWS, SSD_XBC), F32),
            pltpu.VMEM((SSD_L, SSD_D_INNER), F32),
            pltpu.VMEM((SSD_L, SSD_GROUPS * SSD_STATE), BF16),
            pltpu.VMEM((SSD_L, SSD_GROUPS * SSD_STATE), BF16),
            pltpu.VMEM((SSD_GROUPS, SSD_STATE, SSD_GROUP_WIDTH), F32),
        ],
        compiler_params=pltpu.CompilerParams(
            dimension_semantics=("parallel", "arbitrary"), vmem_limit_bytes=VMEM_LIMIT),
        name="ssd",
    )(proj, proj, proj, dt_raw, conv_w, conv_b,
      pad_heads(dt_bias), pad_heads(a_log), dskip_x, ssd_norm_w.reshape(1, SSD_D_INNER),
      jnp.asarray(tril, dtype=BF16), jnp.asarray(expand, dtype=BF16))


MERGE_TM = 256


def _merge_out_kernel(x_ref, osb_ref, ossd_ref, omem_ref, gsb_ref, gssd_ref, gmem_ref, bg_ref,
                      wsb_ref, wssd_ref, wmem_ref, wout_ref, fnw_ref, o_ref):
    def gate(g_ref, k):
        return _sigmoid(g_ref[...].astype(F32) + bg_ref[:, k * D_MODEL:(k + 1) * D_MODEL])

    merged = gate(gsb_ref, 0) * _dot(osb_ref[...], wsb_ref[...])
    merged = merged + gate(gssd_ref, 1) * _dot(ossd_ref[...], wssd_ref[...])
    merged = merged + gate(gmem_ref, 2) * _dot(omem_ref[...], wmem_ref[...])
    y = x_ref[...] + _dot(merged.astype(BF16), wout_ref[...])
    ms = jnp.mean(y * y, axis=-1, keepdims=True)
    o_ref[...] = y * lax.rsqrt(ms + NORM_EPS) * fnw_ref[...]


def _merge_out(xt, o_sb, o_ssd, o_mem, proj, b_gate, w_sb, w_ssd, w_mem, w_out, final_norm_w):
    gb = COL_GATE // D_MODEL
    tile = lambda width: pl.BlockSpec((MERGE_TM, width), lambda i: (i, 0))
    resident = lambda shape: pl.BlockSpec(shape, lambda i: (0, 0), pipeline_mode=pl.Buffered(1))
    return pl.pallas_call(
        _merge_out_kernel,
        grid=(TOKENS // MERGE_TM,),
        in_specs=[
            tile(D_MODEL), tile(SB_WIDTH), tile(SSD_D_INNER), tile(MEM_WIDTH),
            pl.BlockSpec((MERGE_TM, D_MODEL), lambda i: (i, gb)),
            pl.BlockSpec((MERGE_TM, D_MODEL), lambda i: (i, gb + 1)),
            pl.BlockSpec((MERGE_TM, D_MODEL), lambda i: (i, gb + 2)),
            resident((1, 3 * D_MODEL)),
            resident((SB_WIDTH, D_MODEL)), resident((SSD_D_INNER, D_MODEL)),
            resident((MEM_WIDTH, D_MODEL)), resident((D_MODEL, D_MODEL)),
            resident((1, D_MODEL)),
        ],
        out_specs=tile(D_MODEL),
        out_shape=jax.ShapeDtypeStruct((TOKENS, D_MODEL), F32),
        compiler_params=pltpu.CompilerParams(
            dimension_semantics=("parallel",), vmem_limit_bytes=VMEM_LIMIT),
        name="merge_out",
    )(xt, o_sb, o_ssd, o_mem, proj, proj, proj, b_gate, w_sb, w_ssd, w_mem, w_out, final_norm_w)


MEMKV_TN = 512
MEM_TM = 512


def _mem_kv_kernel(m_ref, nw_ref, w_ref, o_ref):
    m = m_ref[...]
    ms = jnp.mean(m * m, axis=-1, keepdims=True)
    mn = (m * lax.rsqrt(ms + NORM_EPS) * nw_ref[...]).astype(BF16)
    o_ref[...] = _dot(mn, w_ref[...]).astype(BF16)


def _mem_kv(mem2, mem_norm_w, w_kv):
    rows = BATCH * MEM_LEN
    return pl.pallas_call(
        _mem_kv_kernel,
        grid=(2 * MEM_WIDTH // MEMKV_TN,),
        in_specs=[
            pl.BlockSpec((rows, D_MODEL), lambda j: (0, 0)),
            pl.BlockSpec((1, D_MODEL), lambda j: (0, 0)),
            pl.BlockSpec((D_MODEL, MEMKV_TN), lambda j: (0, j)),
        ],
        out_specs=pl.BlockSpec((rows, MEMKV_TN), lambda j: (0, j)),
        out_shape=jax.ShapeDtypeStruct((rows, 2 * MEM_WIDTH), BF16),
        compiler_params=pltpu.CompilerParams(
            dimension_semantics=("parallel",), vmem_limit_bytes=VMEM_LIMIT),
        name="mem_kv",
    )(mem2, mem_norm_w, w_kv)


def _mem_attn_kernel(q_ref, z_ref, kv_ref, o_ref):
    scale = MEM_HEAD_DIM ** -0.5
    for hd in range(MEM_HEADS):
        lo, hi = hd * MEM_HEAD_DIM, (hd + 1) * MEM_HEAD_DIM
        s = _dot_nt(q_ref[:, lo:hi], kv_ref[:, lo:hi]) * scale
        p = jnp.exp(s - jnp.max(s, axis=-1, keepdims=True))
        den = jnp.sum(p, axis=-1, keepdims=True)
        o = _dot(p.astype(BF16), kv_ref[:, MEM_WIDTH + lo:MEM_WIDTH + hi]) / den
        z = z_ref[:, lo:hi].astype(F32)
        o_ref[:, lo:hi] = (o * _silu(z)).astype(BF16)


def _mem_attn(proj, kv):
    nt = SEQ // MEM_TM
    qb, zb = COL_MEM_Q // MEM_WIDTH, COL_MEM_Z // MEM_WIDTH
    return pl.pallas_call(
        _mem_attn_kernel,
        grid=(BATCH, nt),
        in_specs=[
            pl.BlockSpec((MEM_TM, MEM_WIDTH), lambda b, i: (b * nt + i, qb)),
            pl.BlockSpec((MEM_TM, MEM_WIDTH), lambda b, i: (b * nt + i, zb)),
            pl.BlockSpec((MEM_LEN, 2 * MEM_WIDTH), lambda b, i: (b, 0)),
        ],
        out_specs=pl.BlockSpec((MEM_TM, MEM_WIDTH), lambda b, i: (b * nt + i, 0)),
        out_shape=jax.ShapeDtypeStruct((TOKENS, MEM_WIDTH), BF16),
        compiler_params=pltpu.CompilerParams(
            dimension_semantics=("parallel", "parallel"), vmem_limit_bytes=VMEM_LIMIT),
        name="mem_attn",
    )(proj, proj, kv)


def kernel(x, mem, norm_w, mem_norm_w, w_in, b_gate, conv_w, conv_b, dt_bias, a_log, d_skip,
           ssd_norm_w, w_mem_kv, w_branch_sb, w_branch_ssd, w_branch_mem, w_out, final_norm_w):
    xt = x.reshape(TOKENS, D_MODEL)
    w_in_t = w_in[0].T

    h, dt_raw, proj = _norm_in(xt, norm_w[0].reshape(1, D_MODEL), w_in_t)
    proj, (w_kv, w_sb, w_ssd, w_mem, w_o) = _in_proj(
        h, w_in_t, proj, [w_mem_kv[0], w_branch_sb[0], w_branch_ssd[0], w_branch_mem[0], w_out[0]])
    o_sb = _sb_attn(proj)
    o_ssd = _ssd(proj, dt_raw, conv_w[0], conv_b[0].reshape(1, -1), dt_bias[0], a_log[0],
                 d_skip[0], ssd_norm_w[0])
    kv = _mem_kv(mem.reshape(BATCH * MEM_LEN, D_MODEL), mem_norm_w[0].reshape(1, D_MODEL), w_kv)
    o_mem = _mem_attn(proj, kv)
    out = _merge_out(xt, o_sb, o_ssd, o_mem, proj, b_gate[0].reshape(1, -1),
                     w_sb, w_ssd, w_mem, w_o, final_norm_w.reshape(1, D_MODEL))
    return out.reshape(BATCH, SEQ, D_MODEL)
```

```python
import numpy as np
import jax
import jax.numpy as jnp
from jax import lax
from jax.experimental import pallas as pl
from jax.experimental.pallas import tpu as pltpu

F32 = jnp.float32
BF16 = jnp.bfloat16

D_MODEL = 2048
BATCH = 2
SEQ = 4096
TOKENS = BATCH * SEQ

SB_HEADS = 8
SB_HEAD_DIM = 128
SB_WIDTH = SB_HEADS * SB_HEAD_DIM

SSD_D_INNER = 2048
SSD_HEAD_DIM = 64
SSD_HEADS = 32
SSD_GROUPS = 8
SSD_HEADS_PER_GROUP = 4
SSD_STATE = 128
SSD_CONV = 4
SSD_GROUP_WIDTH = SSD_HEADS_PER_GROUP * SSD_HEAD_DIM

MEM_LEN = 256
MEM_HEADS = 4
MEM_HEAD_DIM = 256
MEM_WIDTH = MEM_HEADS * MEM_HEAD_DIM

NORM_EPS = 1e-6

COL_SB_Q = 0
COL_SB_K = 1024
COL_SB_V = 2048
COL_SB_Z = 3072
COL_SSD_Z = 4096
COL_SSD_X = 6144
COL_SSD_B = 8192
COL_DT = 10240
DT_WIDTH = 32
COL_MEM_Q = 10240
COL_MEM_Z = 11264
COL_GATE = 12288
PROJ_WIDTH = 18432

LANES = 128
VMEM_LIMIT = 56 * 1024 * 1024


def _sigmoid(v):
    return 0.5 + 0.5 * jnp.tanh(0.5 * v)


def _silu(v):
    h = 0.5 * v
    return h + h * jnp.tanh(h)


def _split2(v):
    hi = v.astype(BF16)
    lo = (v - hi.astype(F32)).astype(BF16)
    return hi, lo


def _split3(v):
    hi = v.astype(BF16)
    r = v - hi.astype(F32)
    mid = r.astype(BF16)
    lo = (r - mid.astype(F32)).astype(BF16)
    return hi, mid, lo


def _dot(a, b):
    return jnp.dot(a, b, preferred_element_type=F32)


def _dot_nt(a, b):
    return lax.dot_general(a, b, (((1,), (1,)), ((), ())), preferred_element_type=F32)


NORM_TM = 1024
NORM_CHUNK = 256
PROJ_TM = 2048
PROJ_TN = 1024


def _norm_in_kernel(x_ref, nw_ref, wdt_ref, wq_ref, h_ref, dt_ref, q_ref, wqb_ref):
    @pl.when(pl.program_id(0) == 0)
    def _():
        wqb_ref[...] = wq_ref[...].astype(BF16)

    wdt = wdt_ref[...].astype(BF16)
    for r in range(NORM_TM // NORM_CHUNK):
        rows = slice(r * NORM_CHUNK, (r + 1) * NORM_CHUNK)
        x = x_ref[rows, :]
        ms = jnp.mean(x * x, axis=-1, keepdims=True)
        hb = (x * lax.rsqrt(ms + NORM_EPS) * nw_ref[...]).astype(BF16)
        h_ref[rows, :] = hb
        dt_ref[rows, :] = _dot_nt(hb, wdt)
        q_ref[rows, :] = (_dot_nt(hb, wqb_ref[...]) * SB_Q_PRESCALE).astype(BF16)


def _norm_in(xt, norm_w, w_in_t):
    assert SB_WIDTH == PROJ_TN and COL_SB_Q == 0
    return pl.pallas_call(
        _norm_in_kernel,
        grid=(TOKENS // NORM_TM,),
        in_specs=[
            pl.BlockSpec((NORM_TM, D_MODEL), lambda i: (i, 0)),
            pl.BlockSpec((1, D_MODEL), lambda i: (0, 0)),
            pl.BlockSpec((LANES, D_MODEL), lambda i: (COL_DT // LANES, 0)),
            pl.BlockSpec((PROJ_TN, D_MODEL), lambda i: (0, 0), pipeline_mode=pl.Buffered(1)),
        ],
        out_specs=[
            pl.BlockSpec((NORM_TM, D_MODEL), lambda i: (i, 0)),
            pl.BlockSpec((NORM_TM, LANES), lambda i: (i, 0)),
            pl.BlockSpec((NORM_TM, PROJ_TN), lambda i: (i, 0)),
        ],
        out_shape=[
            jax.ShapeDtypeStruct((TOKENS, D_MODEL), BF16),
            jax.ShapeDtypeStruct((TOKENS, LANES), F32),
            jax.ShapeDtypeStruct((TOKENS, PROJ_WIDTH), BF16),
        ],
        scratch_shapes=[pltpu.VMEM((PROJ_TN, D_MODEL), BF16)],
        compiler_params=pltpu.CompilerParams(
            dimension_semantics=("arbitrary",), vmem_limit_bytes=VMEM_LIMIT),
        name="norm_in",
    )(xt, norm_w, w_in_t, w_in_t)


PROJ_SHIFT_TILE = COL_DT // PROJ_TN
PROJ_CAST_SLICES = 64
PROJ_VMEM_LIMIT = 60 * 1024 * 1024


def _in_proj_kernel(h_ref, w_ref, wnext_ref, proj_in_ref, *rest):
    del proj_in_ref
    n_side = (len(rest) - 2) // 2
    side_in, o_ref, side_out, wb_ref = rest[:n_side], rest[n_side], rest[n_side + 1:-1], rest[-1]
    for src, dst in zip(side_in, side_out):
        dst[...] = src[...].astype(BF16)

    tile = pl.program_id(0) + 1

    @pl.when(pl.program_id(1) == 0)
    def _():
        @pl.when(tile < PROJ_SHIFT_TILE)
        def _():
            wb_ref[...] = w_ref[...].astype(BF16)

        @pl.when(tile >= PROJ_SHIFT_TILE)
        def _():
            keep = PROJ_TN - DT_WIDTH
            wb_ref[:keep, :] = w_ref[DT_WIDTH:, :].astype(BF16)
            wb_ref[keep:, :] = wnext_ref[...].astype(BF16)

    o_ref[...] = _dot_nt(h_ref[...], wb_ref[...]).astype(BF16)


def _in_proj(h, w_in_t, proj, side_weights):
    next_per_tile = PROJ_TN // DT_WIDTH
    steps_i = TOKENS // PROJ_TM
    grid = (PROJ_WIDTH // PROJ_TN - 1, steps_i)
    assert grid[0] * grid[1] >= PROJ_CAST_SLICES
    slice_map = lambda j, i: (jnp.minimum(j * steps_i + i, PROJ_CAST_SLICES - 1), 0)
    side_specs = [pl.BlockSpec((w.shape[0] // PROJ_CAST_SLICES, D_MODEL), slice_map)
                  for w in side_weights]
    outs = pl.pallas_call(
        _in_proj_kernel,
        grid=grid,
        in_specs=[
            pl.BlockSpec((PROJ_TM, D_MODEL), lambda j, i: (i, 0)),
            pl.BlockSpec((PROJ_TN, D_MODEL), lambda j, i: (j + 1, 0)),
            pl.BlockSpec((DT_WIDTH, D_MODEL), lambda j, i: ((j + 2) * next_per_tile, 0)),
            pl.BlockSpec(memory_space=pl.ANY),
        ] + side_specs,
        out_specs=[pl.BlockSpec((PROJ_TM, PROJ_TN), lambda j, i: (i, j + 1))] + side_specs,
        out_shape=[jax.ShapeDtypeStruct((TOKENS, PROJ_WIDTH), BF16)]
        + [jax.ShapeDtypeStruct(w.shape, BF16) for w in side_weights],
        scratch_shapes=[pltpu.VMEM((PROJ_TN, D_MODEL), BF16)],
        input_output_aliases={3: 0},
        compiler_params=pltpu.CompilerParams(
            dimension_semantics=("arbitrary", "arbitrary"), vmem_limit_bytes=PROJ_VMEM_LIMIT),
        name="in_proj",
    )(h, w_in_t, w_in_t, proj, *side_weights)
    return outs[0], outs[1:]


SB_TQ = 256
SB_TK = 256
SB_NQ = SEQ // SB_TQ
SB_HPS = 8
SB_STEP_WIDTH = SB_HPS * SB_HEAD_DIM
LOG2E = 1.4426950408889634
SB_Q_PRESCALE = -(SB_HEAD_DIM ** -0.5) * LOG2E
SB_DEAD_LOG2 = -160.0


def _sb_log_keep(ny):
    lse = jnp.log(1.0 + jnp.exp2(-jnp.abs(ny))) * LOG2E
    return jnp.minimum(ny, 0.0) - lse


def _sb_suffix(lk, wcs):
    hi, lo = _split2(lk)
    c = _dot(jnp.concatenate([hi, lo], axis=1), wcs)
    return c[:, :LANES], c[:, LANES:]


def _sb_weights(ny, wcs, carry, mask):
    lk = _sb_log_keep(ny)
    if mask is not None:
        lk = jnp.where(mask, lk, 0.0)
    s1, r1 = _sb_suffix(lk[:, LANES:], wcs)
    s0, r0 = _sb_suffix(lk[:, :LANES], wcs)
    base0 = carry + r1
    w = jnp.exp2(jnp.concatenate([s0 + base0, s1 + carry], axis=1) - ny)
    if mask is not None:
        w = jnp.where(mask, w, 0.0)
    return base0 + r0, w.astype(BF16)


def _sb_attn_kernel(q_ref, k_ref, v_ref, z_ref, wcs_ref, o_ref, carry_ref, acc_ref, ny_ref, w_ref):
    i = pl.program_id(2)
    wcs = wcs_ref[...]
    row = lax.broadcasted_iota(jnp.int32, (SB_TQ, SB_TK), 0)
    col = lax.broadcasted_iota(jnp.int32, (SB_TQ, SB_TK), 1)
    heads = [slice(hh * SB_HEAD_DIM, (hh + 1) * SB_HEAD_DIM) for hh in range(SB_HPS)]

    def block_start(kb):
        return pl.multiple_of(kb * SB_TK, SB_TK)

    def qk(kb):
        for hh, sl in enumerate(heads):
            ny_ref[hh] = _dot_nt(q_ref[:, sl], k_ref[pl.ds(block_start(kb), SB_TK), sl])

    def pv(kb):
        for hh, sl in enumerate(heads):
            acc_ref[hh] += _dot(w_ref[hh], v_ref[pl.ds(block_start(kb), SB_TK), sl])

    def weights(mask):
        for hh in range(SB_HPS):
            carry_ref[hh], w_ref[hh] = _sb_weights(ny_ref[hh], wcs, carry_ref[hh], mask)

    def more_blocks(cur):
        top = carry_ref[0]
        for hh in range(1, SB_HPS):
            top = jnp.maximum(top, carry_ref[hh])
        return jnp.logical_and(cur > 0, jnp.max(top) > SB_DEAD_LOG2)

    carry_ref[...] = jnp.zeros(carry_ref.shape, F32)
    acc_ref[...] = jnp.zeros(acc_ref.shape, F32)
    qk(i)
    weights(col < row)
    qk(jnp.maximum(i - 1, 0))

    def body(state):
        last, _ = state
        cur = last - 1
        pv(last)
        weights(None)
        qk(jnp.maximum(cur - 1, 0))
        return cur, more_blocks(cur)

    last, _ = lax.while_loop(lambda state: state[1], body, (i, more_blocks(i)))
    pv(last)
    for hh, sl in enumerate(heads):
        z = z_ref[:, sl].astype(F32)
        o_ref[:, sl] = (acc_ref[hh] * _silu(z)).astype(BF16)


def _sb_cumsum_matrix():
    j = np.arange(2 * LANES)[:, None] % LANES
    s = np.arange(2 * LANES)[None, :]
    m = np.where(s < LANES, (j >= s), True)
    return jnp.asarray(m.astype(np.float32), dtype=BF16)


def _sb_attn(proj):
    w = SB_STEP_WIDTH
    qb, kb, vb, zb = COL_SB_Q // w, COL_SB_K // w, COL_SB_V // w, COL_SB_Z // w
    return pl.pallas_call(
        _sb_attn_kernel,
        grid=(BATCH, SB_HEADS // SB_HPS, SB_NQ),
        in_specs=[
            pl.BlockSpec((SB_TQ, w), lambda b, h, i: (b * SB_NQ + i, qb + h)),
            pl.BlockSpec((SEQ, w), lambda b, h, i: (b, kb + h)),
            pl.BlockSpec((SEQ, w), lambda b, h, i: (b, vb + h)),
            pl.BlockSpec((SB_TQ, w), lambda b, h, i: (b * SB_NQ + i, zb + h)),
            pl.BlockSpec((2 * LANES, 2 * LANES), lambda b, h, i: (0, 0)),
        ],
        out_specs=pl.BlockSpec((SB_TQ, w), lambda b, h, i: (b * SB_NQ + i, h)),
        out_shape=jax.ShapeDtypeStruct((TOKENS, SB_WIDTH), BF16),
        scratch_shapes=[
            pltpu.VMEM((SB_HPS, SB_TQ, LANES), F32),
            pltpu.VMEM((SB_HPS, SB_TQ, LANES), F32),
            pltpu.VMEM((SB_HPS, SB_TQ, SB_TK), F32),
            pltpu.VMEM((SB_HPS, SB_TQ, SB_TK), BF16),
        ],
        compiler_params=pltpu.CompilerParams(
            dimension_semantics=("parallel", "parallel", "arbitrary"),
            vmem_limit_bytes=VMEM_LIMIT),
        name="sb_attn",
    )(proj, proj, proj, proj, _sb_cumsum_matrix())


SSD_L = 128
SSD_STEP_ROWS = 512
SSD_NC = SEQ // SSD_STEP_ROWS
SSD_PAD = 8
SSD_ROWS = SSD_PAD + SSD_STEP_ROWS
SSD_XBC = SSD_D_INNER + 2 * SSD_GROUPS * SSD_STATE
SSD_CONV_PIECE = 512


def _ssd_kernel(x_ref, bc_ref, z_ref, dtraw_ref, cw_ref, cbias_ref, dtb_ref, alog_ref,
                dskip_ref, nw_ref, tril_ref, expand_ref,
                o_ref, pad, xc_s, b_s, c_s, state):
    c = pl.program_id(1)
    L = SSD_L

    @pl.when(c == 0)
    def _():
        pad[0:SSD_PAD, :] = jnp.zeros((SSD_PAD, SSD_XBC), F32)
        state[...] = jnp.zeros(state.shape, F32)

    @pl.when(c > 0)
    def _():
        pad[0:SSD_PAD, :] = pad[SSD_STEP_ROWS:SSD_ROWS, :]

    pad[SSD_PAD:SSD_ROWS, 0:SSD_D_INNER] = x_ref[...].astype(F32)
    pad[SSD_PAD:SSD_ROWS, SSD_D_INNER:SSD_XBC] = bc_ref[...].astype(F32)

    for sub in range(SSD_STEP_ROWS // L):
        _ssd_scan_chunk(sub * L, z_ref, dtraw_ref, cw_ref, cbias_ref, dtb_ref, alog_ref,
                        dskip_ref, nw_ref, tril_ref, expand_ref, o_ref, pad, xc_s, b_s, c_s, state)


def _ssd_scan_chunk(r0, z_ref, dtraw_ref, cw_ref, cbias_ref, dtb_ref, alog_ref,
                    dskip_ref, nw_ref, tril_ref, expand_ref, o_ref, pad, xc_s, b_s, c_s, state):
    L = SSD_L
    rows = slice(r0, r0 + L)

    bc_width = SSD_GROUPS * SSD_STATE
    for p in range(SSD_XBC // SSD_CONV_PIECE):
        lo = p * SSD_CONV_PIECE
        cols = slice(lo, lo + SSD_CONV_PIECE)
        wh = 0.5 * cw_ref[:, cols]
        u = pad[r0:r0 + SSD_PAD + L, cols]
        u1 = pltpu.roll(u, 1, 0)
        older = u1 * wh[0:1, :] + u * wh[1:2, :]
        newer = u1 * wh[2:3, :] + u * wh[3:4, :]
        acc = pltpu.roll(older, 2, 0) + newer
        h = acc[SSD_PAD:SSD_PAD + L, :] + 0.5 * cbias_ref[:, cols]
        v = h + h * jnp.tanh(h)
        if lo < SSD_D_INNER:
            xc_s[:, cols] = v
        elif lo < SSD_D_INNER + bc_width:
            b_s[:, lo - SSD_D_INNER:lo - SSD_D_INNER + SSD_CONV_PIECE] = v.astype(BF16)
        else:
            off = lo - SSD_D_INNER - bc_width
            c_s[:, off:off + SSD_CONV_PIECE] = v.astype(BF16)

    raw = dtraw_ref[rows, :] + dtb_ref[...]
    dt = jnp.maximum(raw, 0.0) + jnp.log1p(jnp.exp(-jnp.abs(raw)))
    adt = dt * (-jnp.exp(alog_ref[...]) * LOG2E)
    tril = tril_ref[...]
    acum = sum(_dot(tril, p) for p in _split3(adt))
    rsrc_t = (acum - jnp.log(dt) * LOG2E).T
    expand = expand_ref[...]
    ea_x = sum(_dot(p, expand) for p in _split2(jnp.exp2(acum)))
    wst = jnp.exp2(acum[L - 1:L, :] - acum) * dt
    wst_x = sum(_dot(p, expand) for p in _split2(wst))

    row = lax.broadcasted_iota(jnp.int32, (L, L), 0)
    col = lax.broadcasted_iota(jnp.int32, (L, L), 1)
    causal = col <= row
    low_half = lax.broadcasted_iota(jnp.int32, (L, LANES), 1) < SSD_HEAD_DIM
    for g in range(SSD_GROUPS):
        gl = slice(g * SSD_GROUP_WIDTH, (g + 1) * SSD_GROUP_WIDTH)
        nl = slice(g * SSD_STATE, (g + 1) * SSD_STATE)
        bg = b_s[:, nl]
        cg = c_s[:, nl]
        cb = _dot_nt(cg, bg)
        xg = xc_s[:, gl]
        halves = []
        for t in range(2):
            xt = xg[:, t * LANES:(t + 1) * LANES]
            acc = None
            for u in range(2):
                hd = g * SSD_HEADS_PER_GROUP + 2 * t + u
                seg = acum[:, hd:hd + 1] - rsrc_t[hd:hd + 1, :]
                m = (cb * jnp.where(causal, jnp.exp2(seg), 0.0)).astype(BF16)
                keep = low_half if u == 0 else jnp.logical_not(low_half)
                d = _dot(m, jnp.where(keep, xt, 0.0).astype(BF16))
                acc = d if acc is None else acc + d
            halves.append(acc)
        y = jnp.concatenate(halves, axis=1)

        st = state[g]
        y = y + _dot(cg, st.astype(BF16)) * ea_x[:, gl]
        state[g] = (st * ea_x[L - 1:L, gl]
                    + _dot(bg.astype(F32).T.astype(BF16), (xg * wst_x[:, gl]).astype(BF16)))

        y = y + dskip_ref[:, gl] * xg
        gated = y * _silu(z_ref[rows, gl].astype(F32))
        ms = jnp.mean(gated * gated, axis=-1, keepdims=True)
        o_ref[rows, gl] = (gated * lax.rsqrt(ms + NORM_EPS) * nw_ref[:, gl]).astype(BF16)


def _ssd(proj, dt_raw, conv_w, conv_b, dt_bias, a_log, d_skip, ssd_norm_w):
    def pad_heads(p):
        return jnp.pad(p.reshape(1, SSD_HEADS), ((0, 0), (0, LANES - SSD_HEADS)))

    expand = np.zeros((LANES, SSD_D_INNER), np.float32)
    for hd in range(SSD_HEADS):
        expand[hd, hd * SSD_HEAD_DIM:(hd + 1) * SSD_HEAD_DIM] = 1.0
    tril = np.tril(np.ones((SSD_L, SSD_L), np.float32))
    dskip_x = jnp.repeat(d_skip, SSD_HEAD_DIM).reshape(1, SSD_D_INNER)

    w = SSD_D_INNER
    xb, bcb, zb = COL_SSD_X // w, COL_SSD_B // w, COL_SSD_Z // w
    rows = lambda b, c: b * SSD_NC + c
    full = lambda b, c: (0, 0)
    return pl.pallas_call(
        _ssd_kernel,
        grid=(BATCH, SSD_NC),
        in_specs=[
            pl.BlockSpec((SSD_STEP_ROWS, w), lambda b, c: (rows(b, c), xb)),
            pl.BlockSpec((SSD_STEP_ROWS, w), lambda b, c: (rows(b, c), bcb)),
            pl.BlockSpec((SSD_STEP_ROWS, w), lambda b, c: (rows(b, c), zb)),
            pl.BlockSpec((SSD_STEP_ROWS, LANES), lambda b, c: (rows(b, c), 0)),
            pl.BlockSpec((SSD_CONV, SSD_XBC), full),
            pl.BlockSpec((1, SSD_XBC), full),
            pl.BlockSpec((1, LANES), full),
            pl.BlockSpec((1, LANES), full),
            pl.BlockSpec((1, w), full),
            pl.BlockSpec((1, w), full),
            pl.BlockSpec((SSD_L, SSD_L), full),
            pl.BlockSpec((LANES, w), full),
        ],
        out_specs=pl.BlockSpec((SSD_STEP_ROWS, w), lambda b, c: (rows(b, c), 0)),
        out_shape=jax.ShapeDtypeStruct((TOKENS, SSD_D_INNER), BF16),
        scratch_shapes=[
            pltpu.VMEM((SSD_ROWS, SSD_XBC), F32),
            pltpu.VMEM((SSD_L, SSD_D_INNER), F32),
            pltpu.VMEM((SSD_L, SSD_GROUPS * SSD_STATE), BF16),
            pltpu.VMEM((SSD_L, SSD_GROUPS * SSD_STATE), BF16),
            pltpu.VMEM((SSD_GROUPS, SSD_STATE, SSD_GROUP_WIDTH), F32),
        ],
        compiler_params=pltpu.CompilerParams(
            dimension_semantics=("parallel", "arbitrary"), vmem_limit_bytes=VMEM_LIMIT),
        name="ssd",
    )(proj, proj, proj, dt_raw, conv_w, conv_b,
      pad_heads(dt_bias), pad_heads(a_log), dskip_x, ssd_norm_w.reshape(1, SSD_D_INNER),
      jnp.asarray(tril, dtype=BF16), jnp.asarray(expand, dtype=BF16))


MERGE_TM = 256


def _merge_out_kernel(x_ref, osb_ref, ossd_ref, omem_ref, gsb_ref, gssd_ref, gmem_ref, bg_ref,
                      wsb_ref, wssd_ref, wmem_ref, wout_ref, fnw_ref, o_ref):
    def gate(g_ref, k):
        return _sigmoid(g_ref[...].astype(F32) + bg_ref[:, k * D_MODEL:(k + 1) * D_MODEL])

    merged = gate(gsb_ref, 0) * _dot(osb_ref[...], wsb_ref[...])
    merged = merged + gate(gssd_ref, 1) * _dot(ossd_ref[...], wssd_ref[...])
    merged = merged + gate(gmem_ref, 2) * _dot(omem_ref[...], wmem_ref[...])
    y = x_ref[...] + _dot(merged.astype(BF16), wout_ref[...])
    ms = jnp.mean(y * y, axis=-1, keepdims=True)
    o_ref[...] = y * lax.rsqrt(ms + NORM_EPS) * fnw_ref[...]


def _merge_out(xt, o_sb, o_ssd, o_mem, proj, b_gate, w_sb, w_ssd, w_mem, w_out, final_norm_w):
    gb = COL_GATE // D_MODEL
    tile = lambda width: pl.BlockSpec((MERGE_TM, width), lambda i: (i, 0))
    resident = lambda shape: pl.BlockSpec(shape, lambda i: (0, 0), pipeline_mode=pl.Buffered(1))
    return pl.pallas_call(
        _merge_out_kernel,
        grid=(TOKENS // MERGE_TM,),
        in_specs=[
            tile(D_MODEL), tile(SB_WIDTH), tile(SSD_D_INNER), tile(MEM_WIDTH),
            pl.BlockSpec((MERGE_TM, D_MODEL), lambda i: (i, gb)),
            pl.BlockSpec((MERGE_TM, D_MODEL), lambda i: (i, gb + 1)),
            pl.BlockSpec((MERGE_TM, D_MODEL), lambda i: (i, gb + 2)),
            resident((1, 3 * D_MODEL)),
            resident((SB_WIDTH, D_MODEL)), resident((SSD_D_INNER, D_MODEL)),
            resident((MEM_WIDTH, D_MODEL)), resident((D_MODEL, D_MODEL)),
            resident((1, D_MODEL)),
        ],
        out_specs=tile(D_MODEL),
        out_shape=jax.ShapeDtypeStruct((TOKENS, D_MODEL), F32),
        compiler_params=pltpu.CompilerParams(
            dimension_semantics=("parallel",), vmem_limit_bytes=VMEM_LIMIT),
        name="merge_out",
    )(xt, o_sb, o_ssd, o_mem, proj, proj, proj, b_gate, w_sb, w_ssd, w_mem, w_out, final_norm_w)


MEMKV_TN = 512
MEM_TM = 1024


def _mem_kv_kernel(m_ref, nw_ref, w_ref, o_ref):
    m = m_ref[...]
    ms = jnp.mean(m * m, axis=-1, keepdims=True)
    mn = (m * lax.rsqrt(ms + NORM_EPS) * nw_ref[...]).astype(BF16)
    o_ref[...] = _dot(mn, w_ref[...]).astype(BF16)


def _mem_kv(mem2, mem_norm_w, w_kv):
    rows = BATCH * MEM_LEN
    return pl.pallas_call(
        _mem_kv_kernel,
        grid=(2 * MEM_WIDTH // MEMKV_TN,),
        in_specs=[
            pl.BlockSpec((rows, D_MODEL), lambda j: (0, 0)),
            pl.BlockSpec((1, D_MODEL), lambda j: (0, 0)),
            pl.BlockSpec((D_MODEL, MEMKV_TN), lambda j: (0, j)),
        ],
        out_specs=pl.BlockSpec((rows, MEMKV_TN), lambda j: (0, j)),
        out_shape=jax.ShapeDtypeStruct((rows, 2 * MEM_WIDTH), BF16),
        compiler_params=pltpu.CompilerParams(
            dimension_semantics=("parallel",), vmem_limit_bytes=VMEM_LIMIT),
        name="mem_kv",
    )(mem2, mem_norm_w, w_kv)


def _mem_attn_kernel(q_ref, z_ref, kv_ref, o_ref):
    scale = MEM_HEAD_DIM ** -0.5
    for hd in range(MEM_HEADS):
        lo, hi = hd * MEM_HEAD_DIM, (hd + 1) * MEM_HEAD_DIM
        s = _dot_nt(q_ref[:, lo:hi], kv_ref[:, lo:hi]) * scale
        p = jnp.exp(s - jnp.max(s, axis=-1, keepdims=True))
        den = jnp.sum(p, axis=-1, keepdims=True)
        o = _dot(p.astype(BF16), kv_ref[:, MEM_WIDTH + lo:MEM_WIDTH + hi]) / den
        z = z_ref[:, lo:hi].astype(F32)
        o_ref[:, lo:hi] = (o * _silu(z)).astype(BF16)


def _mem_attn(proj, kv):
    nt = SEQ // MEM_TM
    qb, zb = COL_MEM_Q // MEM_WIDTH, COL_MEM_Z // MEM_WIDTH
    return pl.pallas_call(
        _mem_attn_kernel,
        grid=(BATCH, nt),
        in_specs=[
            pl.BlockSpec((MEM_TM, MEM_WIDTH), lambda b, i: (b * nt + i, qb)),
            pl.BlockSpec((MEM_TM, MEM_WIDTH), lambda b, i: (b * nt + i, zb)),
            pl.BlockSpec((MEM_LEN, 2 * MEM_WIDTH), lambda b, i: (b, 0)),
        ],
        out_specs=pl.BlockSpec((MEM_TM, MEM_WIDTH), lambda b, i: (b * nt + i, 0)),
        out_shape=jax.ShapeDtypeStruct((TOKENS, MEM_WIDTH), BF16),
        compiler_params=pltpu.CompilerParams(
            dimension_semantics=("parallel", "parallel"), vmem_limit_bytes=VMEM_LIMIT),
        name="mem_attn",
    )(proj, proj, kv)


def kernel(x, mem, norm_w, mem_norm_w, w_in, b_gate, conv_w, conv_b, dt_bias, a_log, d_skip,
           ssd_norm_w, w_mem_kv, w_branch_sb, w_branch_ssd, w_branch_mem, w_out, final_norm_w):
    xt = x.reshape(TOKENS, D_MODEL)
    w_in_t = w_in[0].T

    h, dt_raw, proj = _norm_in(xt, norm_w[0].reshape(1, D_MODEL), w_in_t)
    proj, (w_kv, w_sb, w_ssd, w_mem, w_o) = _in_proj(
        h, w_in_t, proj, [w_mem_kv[0], w_branch_sb[0], w_branch_ssd[0], w_branch_mem[0], w_out[0]])
    o_sb = _sb_attn(proj)
    o_ssd = _ssd(proj, dt_raw, conv_w[0], conv_b[0].reshape(1, -1), dt_bias[0], a_log[0],
                 d_skip[0], ssd_norm_w[0])
    kv = _mem_kv(mem.reshape(BATCH * MEM_LEN, D_MODEL), mem_norm_w[0].reshape(1, D_MODEL), w_kv)
    o_mem = _mem_attn(proj, kv)
    out = _merge_out(xt, o_sb, o_ssd, o_mem, proj, b_gate[0].reshape(1, -1),
                     w_sb, w_ssd, w_mem, w_o, final_norm_w.reshape(1, D_MODEL))
    return out.reshape(BATCH, SEQ, D_MODEL)
```

```python
import numpy as np
import jax
import jax.numpy as jnp
from jax import lax
from jax.experimental import pallas as pl
from jax.experimental.pallas import tpu as pltpu

F32 = jnp.float32
BF16 = jnp.bfloat16

D_MODEL = 2048
BATCH = 2
SEQ = 4096
TOKENS = BATCH * SEQ

SB_HEADS = 8
SB_HEAD_DIM = 128
SB_WIDTH = SB_HEADS * SB_HEAD_DIM

SSD_D_INNER = 2048
SSD_HEAD_DIM = 64
SSD_HEADS = 32
SSD_GROUPS = 8
SSD_HEADS_PER_GROUP = 4
SSD_STATE = 128
SSD_CONV = 4
SSD_GROUP_WIDTH = SSD_HEADS_PER_GROUP * SSD_HEAD_DIM

MEM_LEN = 256
MEM_HEADS = 4
MEM_HEAD_DIM = 256
MEM_WIDTH = MEM_HEADS * MEM_HEAD_DIM

NORM_EPS = 1e-6

COL_SB_Q = 0
COL_SB_K = 1024
COL_SB_V = 2048
COL_SB_Z = 3072
COL_SSD_Z = 4096
COL_SSD_X = 6144
COL_SSD_B = 8192
COL_DT = 10240
DT_WIDTH = 32
COL_MEM_Q = 10240
COL_MEM_Z = 11264
COL_GATE = 12288
PROJ_WIDTH = 18432

LANES = 128
VMEM_LIMIT = 56 * 1024 * 1024


def _sigmoid(v):
    return 0.5 + 0.5 * jnp.tanh(0.5 * v)


def _silu(v):
    h = 0.5 * v
    return h + h * jnp.tanh(h)


def _split2(v):
    hi = v.astype(BF16)
    lo = (v - hi.astype(F32)).astype(BF16)
    return hi, lo


def _split3(v):
    hi = v.astype(BF16)
    r = v - hi.astype(F32)
    mid = r.astype(BF16)
    lo = (r - mid.astype(F32)).astype(BF16)
    return hi, mid, lo


def _dot(a, b):
    return jnp.dot(a, b, preferred_element_type=F32)


def _dot_nt(a, b):
    return lax.dot_general(a, b, (((1,), (1,)), ((), ())), preferred_element_type=F32)


NORM_TM = 1024
NORM_CHUNK = 256
PROJ_TM = 2048
PROJ_TN = 1024


def _norm_in_kernel(x_ref, nw_ref, wdt_ref, wq_ref, h_ref, dt_ref, q_ref, wqb_ref):
    @pl.when(pl.program_id(0) == 0)
    def _():
        wqb_ref[...] = wq_ref[...].astype(BF16)

    wdt = wdt_ref[...].astype(BF16)
    for r in range(NORM_TM // NORM_CHUNK):
        rows = slice(r * NORM_CHUNK, (r + 1) * NORM_CHUNK)
        x = x_ref[rows, :]
        ms = jnp.mean(x * x, axis=-1, keepdims=True)
        hb = (x * lax.rsqrt(ms + NORM_EPS) * nw_ref[...]).astype(BF16)
        h_ref[rows, :] = hb
        dt_ref[rows, :] = _dot_nt(hb, wdt)
        q_ref[rows, :] = (_dot_nt(hb, wqb_ref[...]) * SB_Q_PRESCALE).astype(BF16)


def _norm_in(xt, norm_w, w_in_t):
    assert SB_WIDTH == PROJ_TN and COL_SB_Q == 0
    return pl.pallas_call(
        _norm_in_kernel,
        grid=(TOKENS // NORM_TM,),
        in_specs=[
            pl.BlockSpec((NORM_TM, D_MODEL), lambda i: (i, 0)),
            pl.BlockSpec((1, D_MODEL), lambda i: (0, 0)),
            pl.BlockSpec((LANES, D_MODEL), lambda i: (COL_DT // LANES, 0)),
            pl.BlockSpec((PROJ_TN, D_MODEL), lambda i: (0, 0), pipeline_mode=pl.Buffered(1)),
        ],
        out_specs=[
            pl.BlockSpec((NORM_TM, D_MODEL), lambda i: (i, 0)),
            pl.BlockSpec((NORM_TM, LANES), lambda i: (i, 0)),
            pl.BlockSpec((NORM_TM, PROJ_TN), lambda i: (i, 0)),
        ],
        out_shape=[
            jax.ShapeDtypeStruct((TOKENS, D_MODEL), BF16),
            jax.ShapeDtypeStruct((TOKENS, LANES), F32),
            jax.ShapeDtypeStruct((TOKENS, PROJ_WIDTH), BF16),
        ],
        scratch_shapes=[pltpu.VMEM((PROJ_TN, D_MODEL), BF16)],
        compiler_params=pltpu.CompilerParams(
            dimension_semantics=("arbitrary",), vmem_limit_bytes=VMEM_LIMIT),
        name="norm_in",
    )(xt, norm_w, w_in_t, w_in_t)


PROJ_SHIFT_TILE = COL_DT // PROJ_TN
PROJ_CAST_SLICES = 64
PROJ_VMEM_LIMIT = 60 * 1024 * 1024


def _in_proj_kernel(h_ref, w_ref, wnext_ref, proj_in_ref, *rest):
    del proj_in_ref
    n_side = (len(rest) - 2) // 2
    side_in, o_ref, side_out, wb_ref = rest[:n_side], rest[n_side], rest[n_side + 1:-1], rest[-1]
    for src, dst in zip(side_in, side_out):
        dst[...] = src[...].astype(BF16)

    tile = pl.program_id(0) + 1

    @pl.when(pl.program_id(1) == 0)
    def _():
        @pl.when(tile < PROJ_SHIFT_TILE)
        def _():
            wb_ref[...] = w_ref[...].astype(BF16)

        @pl.when(tile >= PROJ_SHIFT_TILE)
        def _():
            keep = PROJ_TN - DT_WIDTH
            wb_ref[:keep, :] = w_ref[DT_WIDTH:, :].astype(BF16)
            wb_ref[keep:, :] = wnext_ref[...].astype(BF16)

    o_ref[...] = _dot_nt(h_ref[...], wb_ref[...]).astype(BF16)


def _in_proj(h, w_in_t, proj, side_weights):
    next_per_tile = PROJ_TN // DT_WIDTH
    steps_i = TOKENS // PROJ_TM
    grid = (PROJ_WIDTH // PROJ_TN - 1, steps_i)
    assert grid[0] * grid[1] >= PROJ_CAST_SLICES
    slice_map = lambda j, i: (jnp.minimum(j * steps_i + i, PROJ_CAST_SLICES - 1), 0)
    side_specs = [pl.BlockSpec((w.shape[0] // PROJ_CAST_SLICES, D_MODEL), slice_map)
                  for w in side_weights]
    outs = pl.pallas_call(
        _in_proj_kernel,
        grid=grid,
        in_specs=[
            pl.BlockSpec((PROJ_TM, D_MODEL), lambda j, i: (i, 0)),
            pl.BlockSpec((PROJ_TN, D_MODEL), lambda j, i: (j + 1, 0)),
            pl.BlockSpec((DT_WIDTH, D_MODEL), lambda j, i: ((j + 2) * next_per_tile, 0)),
            pl.BlockSpec(memory_space=pl.ANY),
        ] + side_specs,
        out_specs=[pl.BlockSpec((PROJ_TM, PROJ_TN), lambda j, i: (i, j + 1))] + side_specs,
        out_shape=[jax.ShapeDtypeStruct((TOKENS, PROJ_WIDTH), BF16)]
        + [jax.ShapeDtypeStruct(w.shape, BF16) for w in side_weights],
        scratch_shapes=[pltpu.VMEM((PROJ_TN, D_MODEL), BF16)],
        input_output_aliases={3: 0},
        compiler_params=pltpu.CompilerParams(
            dimension_semantics=("arbitrary", "arbitrary"), vmem_limit_bytes=PROJ_VMEM_LIMIT),
        name="in_proj",
    )(h, w_in_t, w_in_t, proj, *side_weights)
    return outs[0], outs[1:]


SB_TQ = 256
SB_TK = 256
SB_NQ = SEQ // SB_TQ
SB_HPS = 8
SB_STEP_WIDTH = SB_HPS * SB_HEAD_DIM
LOG2E = 1.4426950408889634
SB_Q_PRESCALE = -(SB_HEAD_DIM ** -0.5) * LOG2E
SB_DEAD_LOG2 = -140.0
SB_HEAD_ROWS = 160


def _sb_log_keep(ny):
    lse = jnp.log(1.0 + jnp.exp2(-jnp.abs(ny))) * LOG2E
    return jnp.minimum(ny, 0.0) - lse


def _sb_suffix(lk, wcs):
    hi, lo = _split2(lk)
    c = _dot(jnp.concatenate([hi, lo], axis=1), wcs)
    return c[:, :LANES], c[:, LANES:]


def _sb_weights(ny, wcs, carry, mask):
    lk = _sb_log_keep(ny)
    if mask is not None:
        lk = jnp.where(mask, lk, 0.0)
    s1, r1 = _sb_suffix(lk[:, LANES:], wcs)
    s0, r0 = _sb_suffix(lk[:, :LANES], wcs)
    base0 = carry + r1
    w = jnp.exp2(jnp.concatenate([s0 + base0, s1 + carry], axis=1) - ny)
    if mask is not None:
        w = jnp.where(mask, w, 0.0)
    return base0 + r0, w.astype(BF16)


def _sb_attn_kernel(q_ref, k_ref, v_ref, z_ref, wcs_ref, o_ref, carry_ref, acc_ref, ny_ref, w_ref):
    i = pl.program_id(2)
    wcs = wcs_ref[...]
    row = lax.broadcasted_iota(jnp.int32, (SB_TQ, SB_TK), 0)
    col = lax.broadcasted_iota(jnp.int32, (SB_TQ, SB_TK), 1)
    heads = [slice(hh * SB_HEAD_DIM, (hh + 1) * SB_HEAD_DIM) for hh in range(SB_HPS)]

    def block_start(kb):
        return pl.multiple_of(kb * SB_TK, SB_TK)

    def qk(kb, rows):
        for hh, sl in enumerate(heads):
            ny_ref[hh, :rows] = _dot_nt(q_ref[:rows, sl], k_ref[pl.ds(block_start(kb), SB_TK), sl])

    def pv(kb, rows):
        for hh, sl in enumerate(heads):
            acc_ref[hh, :rows] += _dot(w_ref[hh, :rows], v_ref[pl.ds(block_start(kb), SB_TK), sl])

    def weights(mask, rows):
        for hh in range(SB_HPS):
            carry_ref[hh, :rows], w_ref[hh, :rows] = _sb_weights(
                ny_ref[hh, :rows], wcs, carry_ref[hh, :rows], mask)

    def alive(lo, hi):
        top = carry_ref[0, lo:hi]
        for hh in range(1, SB_HPS):
            top = jnp.maximum(top, carry_ref[hh, lo:hi])
        return jnp.max(top) > SB_DEAD_LOG2

    def earlier_blocks(rows):
        qk(i - 1, rows)
        pv(i, SB_TQ)
        weights(None, rows)
        qk(jnp.maximum(i - 2, 0), rows)

        def body(state):
            last, _ = state
            cur = last - 1
            pv(last, rows)
            weights(None, rows)
            qk(jnp.maximum(cur - 1, 0), rows)
            return cur, jnp.logical_and(cur > 0, alive(0, rows))

        first = i - 1
        last, _ = lax.while_loop(lambda state: state[1], body,
                                 (first, jnp.logical_and(first > 0, alive(0, rows))))
        pv(last, rows)

    carry_ref[...] = jnp.zeros(carry_ref.shape, F32)
    acc_ref[...] = jnp.zeros(acc_ref.shape, F32)
    qk(i, SB_TQ)
    weights(col < row, SB_TQ)
    go = jnp.logical_and(i > 0, alive(0, SB_TQ))
    tail_alive = alive(SB_HEAD_ROWS, SB_TQ)

    @pl.when(jnp.logical_not(go))
    def _():
        pv(i, SB_TQ)

    @pl.when(jnp.logical_and(go, tail_alive))
    def _():
        earlier_blocks(SB_TQ)

    @pl.when(jnp.logical_and(go, jnp.logical_not(tail_alive)))
    def _():
        earlier_blocks(SB_HEAD_ROWS)

    for hh, sl in enumerate(heads):
        z = z_ref[:, sl].astype(F32)
        o_ref[:, sl] = (acc_ref[hh] * _silu(z)).astype(BF16)


def _sb_cumsum_matrix():
    j = np.arange(2 * LANES)[:, None] % LANES
    s = np.arange(2 * LANES)[None, :]
    m = np.where(s < LANES, (j >= s), True)
    return jnp.asarray(m.astype(np.float32), dtype=BF16)


def _sb_attn(proj):
    w = SB_STEP_WIDTH
    qb, kb, vb, zb = COL_SB_Q // w, COL_SB_K // w, COL_SB_V // w, COL_SB_Z // w
    return pl.pallas_call(
        _sb_attn_kernel,
        grid=(BATCH, SB_HEADS // SB_HPS, SB_NQ),
        in_specs=[
            pl.BlockSpec((SB_TQ, w), lambda b, h, i: (b * SB_NQ + i, qb + h)),
            pl.BlockSpec((SEQ, w), lambda b, h, i: (b, kb + h)),
            pl.BlockSpec((SEQ, w), lambda b, h, i: (b, vb + h)),
            pl.BlockSpec((SB_TQ, w), lambda b, h, i: (b * SB_NQ + i, zb + h)),
            pl.BlockSpec((2 * LANES, 2 * LANES), lambda b, h, i: (0, 0)),
        ],
        out_specs=pl.BlockSpec((SB_TQ, w), lambda b, h, i: (b * SB_NQ + i, h)),
        out_shape=jax.ShapeDtypeStruct((TOKENS, SB_WIDTH), BF16),
        scratch_shapes=[
            pltpu.VMEM((SB_HPS, SB_TQ, LANES), F32),
            pltpu.VMEM((SB_HPS, SB_TQ, LANES), F32),
            pltpu.VMEM((SB_HPS, SB_TQ, SB_TK), F32),
            pltpu.VMEM((SB_HPS, SB_TQ, SB_TK), BF16),
        ],
        compiler_params=pltpu.CompilerParams(
            dimension_semantics=("parallel", "parallel", "arbitrary"),
            vmem_limit_bytes=VMEM_LIMIT),
        name="sb_attn",
    )(proj, proj, proj, proj, _sb_cumsum_matrix())


SSD_L = 128
SSD_STEP_ROWS = 512
SSD_NC = SEQ // SSD_STEP_ROWS
SSD_PAD = 8
SSD_ROWS = SSD_PAD + SSD_STEP_ROWS
SSD_XBC = SSD_D_INNER + 2 * SSD_GROUPS * SSD_STATE
SSD_CONV_PIECE = 512


def _ssd_kernel(x_ref, bc_ref, z_ref, dtraw_ref, cw_ref, cbias_ref, dtb_ref, alog_ref,
                dskip_ref, nw_ref, tril_ref, expand_ref,
                o_ref, pad, xc_s, b_s, c_s, state):
    c = pl.program_id(1)
    L = SSD_L

    @pl.when(c == 0)
    def _():
        pad[0:SSD_PAD, :] = jnp.zeros((SSD_PAD, SSD_XBC), F32)
        state[...] = jnp.zeros(state.shape, F32)

    @pl.when(c > 0)
    def _():
        pad[0:SSD_PAD, :] = pad[SSD_STEP_ROWS:SSD_ROWS, :]

    pad[SSD_PAD:SSD_ROWS, 0:SSD_D_INNER] = x_ref[...].astype(F32)
    pad[SSD_PAD:SSD_ROWS, SSD_D_INNER:SSD_XBC] = bc_ref[...].astype(F32)

    for sub in range(SSD_STEP_ROWS // L):
        _ssd_scan_chunk(sub * L, z_ref, dtraw_ref, cw_ref, cbias_ref, dtb_ref, alog_ref,
                        dskip_ref, nw_ref, tril_ref, expand_ref, o_ref, pad, xc_s, b_s, c_s, state)


def _ssd_scan_chunk(r0, z_ref, dtraw_ref, cw_ref, cbias_ref, dtb_ref, alog_ref,
                    dskip_ref, nw_ref, tril_ref, expand_ref, o_ref, pad, xc_s, b_s, c_s, state):
    L = SSD_L
    rows = slice(r0, r0 + L)

    bc_width = SSD_GROUPS * SSD_STATE
    for p in range(SSD_XBC // SSD_CONV_PIECE):
        lo = p * SSD_CONV_PIECE
        cols = slice(lo, lo + SSD_CONV_PIECE)
        wh = 0.5 * cw_ref[:, cols]
        u = pad[r0:r0 + SSD_PAD + L, cols]
        u1 = pltpu.roll(u, 1, 0)
        older = u1 * wh[0:1, :] + u * wh[1:2, :]
        newer = u1 * wh[2:3, :] + u * wh[3:4, :]
        acc = pltpu.roll(older, 2, 0) + newer
        h = acc[SSD_PAD:SSD_PAD + L, :] + 0.5 * cbias_ref[:, cols]
        v = h + h * jnp.tanh(h)
        if lo < SSD_D_INNER:
            xc_s[:, cols] = v
        elif lo < SSD_D_INNER + bc_width:
            b_s[:, lo - SSD_D_INNER:lo - SSD_D_INNER + SSD_CONV_PIECE] = v.astype(BF16)
        else:
            off = lo - SSD_D_INNER - bc_width
            c_s[:, off:off + SSD_CONV_PIECE] = v.astype(BF16)

    raw = dtraw_ref[rows, :] + dtb_ref[...]
    dt = jnp.maximum(raw, 0.0) + jnp.log1p(jnp.exp(-jnp.abs(raw)))
    adt = dt * (-jnp.exp(alog_ref[...]) * LOG2E)
    tril = tril_ref[...]
    acum = sum(_dot(tril, p) for p in _split3(adt))
    rsrc_t = (acum - jnp.log(dt) * LOG2E).T
    expand = expand_ref[...]
    ea_x = sum(_dot(p, expand) for p in _split2(jnp.exp2(acum)))
    wst = jnp.exp2(acum[L - 1:L, :] - acum) * dt
    wst_x = sum(_dot(p, expand) for p in _split2(wst))

    row = lax.broadcasted_iota(jnp.int32, (L, L), 0)
    col = lax.broadcasted_iota(jnp.int32, (L, L), 1)
    causal = col <= row
    low_half = lax.broadcasted_iota(jnp.int32, (L, LANES), 1) < SSD_HEAD_DIM
    for g in range(SSD_GROUPS):
        gl = slice(g * SSD_GROUP_WIDTH, (g + 1) * SSD_GROUP_WIDTH)
        nl = slice(g * SSD_STATE, (g + 1) * SSD_STATE)
        bg = b_s[:, nl]
        cg = c_s[:, nl]
        cb = _dot_nt(cg, bg)
        xg = xc_s[:, gl]
        halves = []
        for t in range(2):
            xt = xg[:, t * LANES:(t + 1) * LANES]
            acc = None
            for u in range(2):
                hd = g * SSD_HEADS_PER_GROUP + 2 * t + u
                seg = acum[:, hd:hd + 1] - rsrc_t[hd:hd + 1, :]
                m = (cb * jnp.where(causal, jnp.exp2(seg), 0.0)).astype(BF16)
                keep = low_half if u == 0 else jnp.logical_not(low_half)
                d = _dot(m, jnp.where(keep, xt, 0.0).astype(BF16))
                acc = d if acc is None else acc + d
            halves.append(acc)
        y = jnp.concatenate(halves, axis=1)

        st = state[g]
        y = y + _dot(cg, st.astype(BF16)) * ea_x[:, gl]
        state[g] = (st * ea_x[L - 1:L, gl]
                    + _dot(bg.astype(F32).T.astype(BF16), (xg * wst_x[:, gl]).astype(BF16)))

        y = y + dskip_ref[:, gl] * xg
        gated = y * _silu(z_ref[rows, gl].astype(F32))
        ms = jnp.mean(gated * gated, axis=-1, keepdims=True)
        o_ref[rows, gl] = (gated * lax.rsqrt(ms + NORM_EPS) * nw_ref[:, gl]).astype(BF16)


def _ssd(proj, dt_raw, conv_w, conv_b, dt_bias, a_log, d_skip, ssd_norm_w):
    def pad_heads(p):
        return jnp.pad(p.reshape(1, SSD_HEADS), ((0, 0), (0, LANES - SSD_HEADS)))

    expand = np.zeros((LANES, SSD_D_INNER), np.float32)
    for hd in range(SSD_HEADS):
        expand[hd, hd * SSD_HEAD_DIM:(hd + 1) * SSD_HEAD_DIM] = 1.0
    tril = np.tril(np.ones((SSD_L, SSD_L), np.float32))
    dskip_x = jnp.repeat(d_skip, SSD_HEAD_DIM).reshape(1, SSD_D_INNER)

    w = SSD_D_INNER
    xb, bcb, zb = COL_SSD_X // w, COL_SSD_B // w, COL_SSD_Z // w
    rows = lambda b, c: b * SSD_NC + c
    full = lambda b, c: (0, 0)
    return pl.pallas_call(
        _ssd_kernel,
        grid=(BATCH, SSD_NC),
        in_specs=[
            pl.BlockSpec((SSD_STEP_ROWS, w), lambda b, c: (rows(b, c), xb)),
            pl.BlockSpec((SSD_STEP_ROWS, w), lambda b, c: (rows(b, c), bcb)),
            pl.BlockSpec((SSD_STEP_ROWS, w), lambda b, c: (rows(b, c), zb)),
            pl.BlockSpec((SSD_STEP_ROWS, LANES), lambda b, c: (rows(b, c), 0)),
            pl.BlockSpec((SSD_CONV, SSD_XBC), full),
            pl.BlockSpec((1, SSD_XBC), full),
            pl.BlockSpec((1, LANES), full),
            pl.BlockSpec((1, LANES), full),
            pl.BlockSpec((1, w), full),
            pl.BlockSpec((1, w), full),
            pl.BlockSpec((SSD_L, SSD_L), full),
            pl.BlockSpec((LANES, w), full),
        ],
        out_specs=pl.BlockSpec((SSD_STEP_ROWS, w), lambda b, c: (rows(b, c), 0)),
        out_shape=jax.ShapeDtypeStruct((TOKENS, SSD_D_INNER), BF16),
        scratch_shapes=[
            pltpu.VMEM((SSD_ROWS, SSD_XBC), F32),
            pltpu.VMEM((SSD_L, SSD_D_INNER), F32),
            pltpu.VMEM((SSD_L, SSD_GROUPS * SSD_STATE), BF16),
            pltpu.VMEM((SSD_L, SSD_GROUPS * SSD_STATE), BF16),
            pltpu.VMEM((SSD_GROUPS, SSD_STATE, SSD_GROUP_WIDTH), F32),
        ],
        compiler_params=pltpu.CompilerParams(
            dimension_semantics=("parallel", "arbitrary"), vmem_limit_bytes=VMEM_LIMIT),
        name="ssd",
    )(proj, proj, proj, dt_raw, conv_w, conv_b,
      pad_heads(dt_bias), pad_heads(a_log), dskip_x, ssd_norm_w.reshape(1, SSD_D_INNER),
      jnp.asarray(tril, dtype=BF16), jnp.asarray(expand, dtype=BF16))


MERGE_TM = 256


def _merge_out_kernel(x_ref, osb_ref, ossd_ref, omem_ref, gsb_ref, gssd_ref, gmem_ref, bg_ref,
                      wsb_ref, wssd_ref, wmem_ref, wout_ref, fnw_ref, o_ref):
    def gate(g_ref, k):
        return _sigmoid(g_ref[...].astype(F32) + bg_ref[:, k * D_MODEL:(k + 1) * D_MODEL])

    merged = gate(gsb_ref, 0) * _dot(osb_ref[...], wsb_ref[...])
    merged = merged + gate(gssd_ref, 1) * _dot(ossd_ref[...], wssd_ref[...])
    merged = merged + gate(gmem_ref, 2) * _dot(omem_ref[...], wmem_ref[...])
    y = x_ref[...] + _dot(merged.astype(BF16), wout_ref[...])
    ms = jnp.mean(y * y, axis=-1, keepdims=True)
    o_ref[...] = y * lax.rsqrt(ms + NORM_EPS) * fnw_ref[...]


def _merge_out(xt, o_sb, o_ssd, o_mem, proj, b_gate, w_sb, w_ssd, w_mem, w_out, final_norm_w):
    gb = COL_GATE // D_MODEL
    tile = lambda width: pl.BlockSpec((MERGE_TM, width), lambda i: (i, 0))
    resident = lambda shape: pl.BlockSpec(shape, lambda i: (0, 0), pipeline_mode=pl.Buffered(1))
    return pl.pallas_call(
        _merge_out_kernel,
        grid=(TOKENS // MERGE_TM,),
        in_specs=[
            tile(D_MODEL), tile(SB_WIDTH), tile(SSD_D_INNER), tile(MEM_WIDTH),
            pl.BlockSpec((MERGE_TM, D_MODEL), lambda i: (i, gb)),
            pl.BlockSpec((MERGE_TM, D_MODEL), lambda i: (i, gb + 1)),
            pl.BlockSpec((MERGE_TM, D_MODEL), lambda i: (i, gb + 2)),
            resident((1, 3 * D_MODEL)),
            resident((SB_WIDTH, D_MODEL)), resident((SSD_D_INNER, D_MODEL)),
            resident((MEM_WIDTH, D_MODEL)), resident((D_MODEL, D_MODEL)),
            resident((1, D_MODEL)),
        ],
        out_specs=tile(D_MODEL),
        out_shape=jax.ShapeDtypeStruct((TOKENS, D_MODEL), F32),
        compiler_params=pltpu.CompilerParams(
            dimension_semantics=("parallel",), vmem_limit_bytes=VMEM_LIMIT),
        name="merge_out",
    )(xt, o_sb, o_ssd, o_mem, proj, proj, proj, b_gate, w_sb, w_ssd, w_mem, w_out, final_norm_w)


MEMKV_TN = 512
MEM_TM = 1024


def _mem_kv_kernel(m_ref, nw_ref, w_ref, o_ref):
    m = m_ref[...]
    ms = jnp.mean(m * m, axis=-1, keepdims=True)
    mn = (m * lax.rsqrt(ms + NORM_EPS) * nw_ref[...]).astype(BF16)
    o_ref[...] = _dot(mn, w_ref[...]).astype(BF16)


def _mem_kv(mem2, mem_norm_w, w_kv):
    rows = BATCH * MEM_LEN
    return pl.pallas_call(
        _mem_kv_kernel,
        grid=(2 * MEM_WIDTH // MEMKV_TN,),
        in_specs=[
            pl.BlockSpec((rows, D_MODEL), lambda j: (0, 0)),
            pl.BlockSpec((1, D_MODEL), lambda j: (0, 0)),
            pl.BlockSpec((D_MODEL, MEMKV_TN), lambda j: (0, j)),
        ],
        out_specs=pl.BlockSpec((rows, MEMKV_TN), lambda j: (0, j)),
        out_shape=jax.ShapeDtypeStruct((rows, 2 * MEM_WIDTH), BF16),
        compiler_params=pltpu.CompilerParams(
            dimension_semantics=("parallel",), vmem_limit_bytes=VMEM_LIMIT),
        name="mem_kv",
    )(mem2, mem_norm_w, w_kv)


def _mem_attn_kernel(q_ref, z_ref, kv_ref, o_ref):
    scale = MEM_HEAD_DIM ** -0.5
    for hd in range(MEM_HEADS):
        lo, hi = hd * MEM_HEAD_DIM, (hd + 1) * MEM_HEAD_DIM
        s = _dot_nt(q_ref[:, lo:hi], kv_ref[:, lo:hi]) * scale
        p = jnp.exp(s - jnp.max(s, axis=-1, keepdims=True))
        den = jnp.sum(p, axis=-1, keepdims=True)
        o = _dot(p.astype(BF16), kv_ref[:, MEM_WIDTH + lo:MEM_WIDTH + hi]) / den
        z = z_ref[:, lo:hi].astype(F32)
        o_ref[:, lo:hi] = (o * _silu(z)).astype(BF16)


def _mem_attn(proj, kv):
    nt = SEQ // MEM_TM
    qb, zb = COL_MEM_Q // MEM_WIDTH, COL_MEM_Z // MEM_WIDTH
    return pl.pallas_call(
        _mem_attn_kernel,
        grid=(BATCH, nt),
        in_specs=[
            pl.BlockSpec((MEM_TM, MEM_WIDTH), lambda b, i: (b * nt + i, qb)),
            pl.BlockSpec((MEM_TM, MEM_WIDTH), lambda b, i: (b * nt + i, zb)),
            pl.BlockSpec((MEM_LEN, 2 * MEM_WIDTH), lambda b, i: (b, 0)),
        ],
        out_specs=pl.BlockSpec((MEM_TM, MEM_WIDTH), lambda b, i: (b * nt + i, 0)),
        out_shape=jax.ShapeDtypeStruct((TOKENS, MEM_WIDTH), BF16),
        compiler_params=pltpu.CompilerParams(
            dimension_semantics=("parallel", "parallel"), vmem_limit_bytes=VMEM_LIMIT),
        name="mem_attn",
    )(proj, proj, kv)


def kernel(x, mem, norm_w, mem_norm_w, w_in, b_gate, conv_w, conv_b, dt_bias, a_log, d_skip,
           ssd_norm_w, w_mem_kv, w_branch_sb, w_branch_ssd, w_branch_mem, w_out, final_norm_w):
    xt = x.reshape(TOKENS, D_MODEL)
    w_in_t = w_in[0].T

    h, dt_raw, proj = _norm_in(xt, norm_w[0].reshape(1, D_MODEL), w_in_t)
    proj, (w_kv, w_sb, w_ssd, w_mem, w_o) = _in_proj(
        h, w_in_t, proj, [w_mem_kv[0], w_branch_sb[0], w_branch_ssd[0], w_branch_mem[0], w_out[0]])
    o_sb = _sb_attn(proj)
    o_ssd = _ssd(proj, dt_raw, conv_w[0], conv_b[0].reshape(1, -1), dt_bias[0], a_log[0],
                 d_skip[0], ssd_norm_w[0])
    kv = _mem_kv(mem.reshape(BATCH * MEM_LEN, D_MODEL), mem_norm_w[0].reshape(1, D_MODEL), w_kv)
    o_mem = _mem_attn(proj, kv)
    out = _merge_out(xt, o_sb, o_ssd, o_mem, proj, b_gate[0].reshape(1, -1),
                     w_sb, w_ssd, w_mem, w_o, final_norm_w.reshape(1, D_MODEL))
    return out.reshape(BATCH, SEQ, D_MODEL)
```

```python
import numpy as np
import jax
import jax.numpy as jnp
from jax import lax
from jax.experimental import pallas as pl
from jax.experimental.pallas import tpu as pltpu

F32 = jnp.float32
BF16 = jnp.bfloat16

D_MODEL = 2048
BATCH = 2
SEQ = 4096
TOKENS = BATCH * SEQ

SB_HEADS = 8
SB_HEAD_DIM = 128
SB_WIDTH = SB_HEADS * SB_HEAD_DIM

SSD_D_INNER = 2048
SSD_HEAD_DIM = 64
SSD_HEADS = 32
SSD_GROUPS = 8
SSD_HEADS_PER_GROUP = 4
SSD_STATE = 128
SSD_CONV = 4
SSD_GROUP_WIDTH = SSD_HEADS_PER_GROUP * SSD_HEAD_DIM

MEM_LEN = 256
MEM_HEADS = 4
MEM_HEAD_DIM = 256
MEM_WIDTH = MEM_HEADS * MEM_HEAD_DIM

NORM_EPS = 1e-6

COL_SB_Q = 0
COL_SB_K = 1024
COL_SB_V = 2048
COL_SB_Z = 3072
COL_SSD_Z = 4096
COL_SSD_X = 6144
COL_SSD_B = 8192
COL_DT = 10240
DT_WIDTH = 32
COL_MEM_Q = 10240
COL_MEM_Z = 11264
COL_GATE = 12288
PROJ_WIDTH = 18432

LANES = 128
VMEM_LIMIT = 56 * 1024 * 1024


def _sigmoid(v):
    return 0.5 + 0.5 * jnp.tanh(0.5 * v)


def _silu(v):
    h = 0.5 * v
    return h + h * jnp.tanh(h)


def _split2(v):
    hi = v.astype(BF16)
    lo = (v - hi.astype(F32)).astype(BF16)
    return hi, lo


def _split3(v):
    hi = v.astype(BF16)
    r = v - hi.astype(F32)
    mid = r.astype(BF16)
    lo = (r - mid.astype(F32)).astype(BF16)
    return hi, mid, lo


def _dot(a, b):
    return jnp.dot(a, b, preferred_element_type=F32)


def _dot_nt(a, b):
    return lax.dot_general(a, b, (((1,), (1,)), ((), ())), preferred_element_type=F32)


NORM_TM = 1024
NORM_CHUNK = 256
PROJ_TM = 2048
PROJ_TN = 1024


def _norm_in_kernel(x_ref, nw_ref, wdt_ref, wq_ref, h_ref, dt_ref, q_ref, wqb_ref):
    @pl.when(pl.program_id(0) == 0)
    def _():
        wqb_ref[...] = wq_ref[...].astype(BF16)

    wdt = wdt_ref[...].astype(BF16)
    for r in range(NORM_TM // NORM_CHUNK):
        rows = slice(r * NORM_CHUNK, (r + 1) * NORM_CHUNK)
        x = x_ref[rows, :]
        ms = jnp.mean(x * x, axis=-1, keepdims=True)
        hb = (x * lax.rsqrt(ms + NORM_EPS) * nw_ref[...]).astype(BF16)
        h_ref[rows, :] = hb
        dt_ref[rows, :] = _dot_nt(hb, wdt)
        q_ref[rows, :] = (_dot_nt(hb, wqb_ref[...]) * SB_Q_PRESCALE).astype(BF16)


def _norm_in(xt, norm_w, w_in_t):
    assert SB_WIDTH == PROJ_TN and COL_SB_Q == 0
    return pl.pallas_call(
        _norm_in_kernel,
        grid=(TOKENS // NORM_TM,),
        in_specs=[
            pl.BlockSpec((NORM_TM, D_MODEL), lambda i: (i, 0)),
            pl.BlockSpec((1, D_MODEL), lambda i: (0, 0)),
            pl.BlockSpec((LANES, D_MODEL), lambda i: (COL_DT // LANES, 0)),
            pl.BlockSpec((PROJ_TN, D_MODEL), lambda i: (0, 0), pipeline_mode=pl.Buffered(1)),
        ],
        out_specs=[
            pl.BlockSpec((NORM_TM, D_MODEL), lambda i: (i, 0)),
            pl.BlockSpec((NORM_TM, LANES), lambda i: (i, 0)),
            pl.BlockSpec((NORM_TM, PROJ_TN), lambda i: (i, 0)),
        ],
        out_shape=[
            jax.ShapeDtypeStruct((TOKENS, D_MODEL), BF16),
            jax.ShapeDtypeStruct((TOKENS, LANES), F32),
            jax.ShapeDtypeStruct((TOKENS, PROJ_WIDTH), BF16),
        ],
        scratch_shapes=[pltpu.VMEM((PROJ_TN, D_MODEL), BF16)],
        compiler_params=pltpu.CompilerParams(
            dimension_semantics=("arbitrary",), vmem_limit_bytes=VMEM_LIMIT),
        name="norm_in",
    )(xt, norm_w, w_in_t, w_in_t)


PROJ_SHIFT_TILE = COL_DT // PROJ_TN
PROJ_CAST_SLICES = 64
PROJ_VMEM_LIMIT = 60 * 1024 * 1024


def _in_proj_kernel(h_ref, w_ref, wnext_ref, proj_in_ref, *rest):
    del proj_in_ref
    n_side = (len(rest) - 2) // 2
    side_in, o_ref, side_out, wb_ref = rest[:n_side], rest[n_side], rest[n_side + 1:-1], rest[-1]
    for src, dst in zip(side_in, side_out):
        dst[...] = src[...].astype(BF16)

    tile = pl.program_id(0) + 1

    @pl.when(pl.program_id(1) == 0)
    def _():
        @pl.when(tile < PROJ_SHIFT_TILE)
        def _():
            wb_ref[...] = w_ref[...].astype(BF16)

        @pl.when(tile >= PROJ_SHIFT_TILE)
        def _():
            keep = PROJ_TN - DT_WIDTH
            wb_ref[:keep, :] = w_ref[DT_WIDTH:, :].astype(BF16)
            wb_ref[keep:, :] = wnext_ref[...].astype(BF16)

    o_ref[...] = _dot_nt(h_ref[...], wb_ref[...]).astype(BF16)


def _in_proj(h, w_in_t, proj, side_weights):
    next_per_tile = PROJ_TN // DT_WIDTH
    steps_i = TOKENS // PROJ_TM
    grid = (PROJ_WIDTH // PROJ_TN - 1, steps_i)
    assert grid[0] * grid[1] >= PROJ_CAST_SLICES
    slice_map = lambda j, i: (jnp.minimum(j * steps_i + i, PROJ_CAST_SLICES - 1), 0)
    side_specs = [pl.BlockSpec((w.shape[0] // PROJ_CAST_SLICES, D_MODEL), slice_map)
                  for w in side_weights]
    outs = pl.pallas_call(
        _in_proj_kernel,
        grid=grid,
        in_specs=[
            pl.BlockSpec((PROJ_TM, D_MODEL), lambda j, i: (i, 0)),
            pl.BlockSpec((PROJ_TN, D_MODEL), lambda j, i: (j + 1, 0)),
            pl.BlockSpec((DT_WIDTH, D_MODEL), lambda j, i: ((j + 2) * next_per_tile, 0)),
            pl.BlockSpec(memory_space=pl.ANY),
        ] + side_specs,
        out_specs=[pl.BlockSpec((PROJ_TM, PROJ_TN), lambda j, i: (i, j + 1))] + side_specs,
        out_shape=[jax.ShapeDtypeStruct((TOKENS, PROJ_WIDTH), BF16)]
        + [jax.ShapeDtypeStruct(w.shape, BF16) for w in side_weights],
        scratch_shapes=[pltpu.VMEM((PROJ_TN, D_MODEL), BF16)],
        input_output_aliases={3: 0},
        compiler_params=pltpu.CompilerParams(
            dimension_semantics=("arbitrary", "arbitrary"), vmem_limit_bytes=PROJ_VMEM_LIMIT),
        name="in_proj",
    )(h, w_in_t, w_in_t, proj, *side_weights)
    return outs[0], outs[1:]


SB_TQ = 256
SB_TK = 256
SB_NQ = SEQ // SB_TQ
SB_HPS = 8
SB_STEP_WIDTH = SB_HPS * SB_HEAD_DIM
LOG2E = 1.4426950408889634
SB_Q_PRESCALE = -(SB_HEAD_DIM ** -0.5) * LOG2E
SB_DEAD_LOG2 = -140.0
SB_HEAD_ROWS = 160


def _sb_log_keep(ny):
    lse = jnp.log(1.0 + jnp.exp2(-jnp.abs(ny))) * LOG2E
    return jnp.minimum(ny, 0.0) - lse


def _sb_suffix(lk, wcs):
    hi, lo = _split2(lk)
    c = _dot(jnp.concatenate([hi, lo], axis=1), wcs)
    return c[:, :LANES], c[:, LANES:]


def _sb_weights(ny, wcs, carry, mask):
    lk = _sb_log_keep(ny)
    if mask is not None:
        lk = jnp.where(mask, lk, 0.0)
    s1, r1 = _sb_suffix(lk[:, LANES:], wcs)
    s0, r0 = _sb_suffix(lk[:, :LANES], wcs)
    base0 = carry + r1
    w = jnp.exp2(jnp.concatenate([s0 + base0, s1 + carry], axis=1) - ny)
    if mask is not None:
        w = jnp.where(mask, w, 0.0)
    return base0 + r0, w.astype(BF16)


def _sb_attn_kernel(q_ref, k_ref, v_ref, z_ref, wcs_ref, o_ref, carry_ref, acc_ref, ny_ref, w_ref):
    i = pl.program_id(2)
    wcs = wcs_ref[...]
    row = lax.broadcasted_iota(jnp.int32, (SB_TQ, SB_TK), 0)
    col = lax.broadcasted_iota(jnp.int32, (SB_TQ, SB_TK), 1)
    heads = [slice(hh * SB_HEAD_DIM, (hh + 1) * SB_HEAD_DIM) for hh in range(SB_HPS)]

    def block_start(kb):
        return pl.multiple_of(kb * SB_TK, SB_TK)

    def qk(kb, rows):
        for hh, sl in enumerate(heads):
            ny_ref[hh, :rows] = _dot_nt(q_ref[:rows, sl], k_ref[pl.ds(block_start(kb), SB_TK), sl])

    def pv(kb, rows):
        for hh, sl in enumerate(heads):
            acc_ref[hh, :rows] += _dot(w_ref[hh, :rows], v_ref[pl.ds(block_start(kb), SB_TK), sl])

    def weights(mask, rows):
        for hh in range(SB_HPS):
            carry_ref[hh, :rows], w_ref[hh, :rows] = _sb_weights(
                ny_ref[hh, :rows], wcs, carry_ref[hh, :rows], mask)

    def alive(lo, hi):
        top = carry_ref[0, lo:hi]
        for hh in range(1, SB_HPS):
            top = jnp.maximum(top, carry_ref[hh, lo:hi])
        return jnp.max(top) > SB_DEAD_LOG2

    def earlier_blocks(rows):
        qk(i - 1, rows)
        pv(i, SB_TQ)
        weights(None, rows)
        qk(jnp.maximum(i - 2, 0), rows)

        def body(state):
            last, _ = state
            cur = last - 1
            pv(last, rows)
            weights(None, rows)
            qk(jnp.maximum(cur - 1, 0), rows)
            return cur, jnp.logical_and(cur > 0, alive(0, rows))

        first = i - 1
        last, _ = lax.while_loop(lambda state: state[1], body,
                                 (first, jnp.logical_and(first > 0, alive(0, rows))))
        pv(last, rows)

    carry_ref[...] = jnp.zeros(carry_ref.shape, F32)
    acc_ref[...] = jnp.zeros(acc_ref.shape, F32)
    qk(i, SB_TQ)
    weights(col < row, SB_TQ)
    go = jnp.logical_and(i > 0, alive(0, SB_TQ))
    tail_alive = alive(SB_HEAD_ROWS, SB_TQ)

    @pl.when(jnp.logical_not(go))
    def _():
        pv(i, SB_TQ)

    @pl.when(jnp.logical_and(go, tail_alive))
    def _():
        earlier_blocks(SB_TQ)

    @pl.when(jnp.logical_and(go, jnp.logical_not(tail_alive)))
    def _():
        earlier_blocks(SB_HEAD_ROWS)

    for hh, sl in enumerate(heads):
        z = z_ref[:, sl].astype(F32)
        o_ref[:, sl] = (acc_ref[hh] * _silu(z)).astype(BF16)


def _sb_cumsum_matrix():
    j = np.arange(2 * LANES)[:, None] % LANES
    s = np.arange(2 * LANES)[None, :]
    m = np.where(s < LANES, (j >= s), True)
    return jnp.asarray(m.astype(np.float32), dtype=BF16)


def _sb_attn(proj):
    w = SB_STEP_WIDTH
    qb, kb, vb, zb = COL_SB_Q // w, COL_SB_K // w, COL_SB_V // w, COL_SB_Z // w
    return pl.pallas_call(
        _sb_attn_kernel,
        grid=(BATCH, SB_HEADS // SB_HPS, SB_NQ),
        in_specs=[
            pl.BlockSpec((SB_TQ, w), lambda b, h, i: (b * SB_NQ + i, qb + h)),
            pl.BlockSpec((SEQ, w), lambda b, h, i: (b, kb + h)),
            pl.BlockSpec((SEQ, w), lambda b, h, i: (b, vb + h)),
            pl.BlockSpec((SB_TQ, w), lambda b, h, i: (b * SB_NQ + i, zb + h)),
            pl.BlockSpec((2 * LANES, 2 * LANES), lambda b, h, i: (0, 0)),
        ],
        out_specs=pl.BlockSpec((SB_TQ, w), lambda b, h, i: (b * SB_NQ + i, h)),
        out_shape=jax.ShapeDtypeStruct((TOKENS, SB_WIDTH), BF16),
        scratch_shapes=[
            pltpu.VMEM((SB_HPS, SB_TQ, LANES), F32),
            pltpu.VMEM((SB_HPS, SB_TQ, LANES), F32),
            pltpu.VMEM((SB_HPS, SB_TQ, SB_TK), F32),
            pltpu.VMEM((SB_HPS, SB_TQ, SB_TK), BF16),
        ],
        compiler_params=pltpu.CompilerParams(
            dimension_semantics=("parallel", "parallel", "arbitrary"),
            vmem_limit_bytes=VMEM_LIMIT),
        name="sb_attn",
    )(proj, proj, proj, proj, _sb_cumsum_matrix())


SSD_L = 128
SSD_STEP_ROWS = 512
SSD_NC = SEQ // SSD_STEP_ROWS
SSD_PAD = 8
SSD_ROWS = SSD_PAD + SSD_STEP_ROWS
SSD_XBC = SSD_D_INNER + 2 * SSD_GROUPS * SSD_STATE
SSD_CONV_PIECE = 512


def _ssd_kernel(x_ref, bc_ref, z_ref, dtraw_ref, cw_ref, cbias_ref, dtb_ref, alog_ref,
                dskip_ref, nw_ref, tril_ref, expand_ref,
                o_ref, pad, xc_s, b_s, c_s, state):
    c = pl.program_id(1)
    L = SSD_L

    @pl.when(c == 0)
    def _():
        pad[0:SSD_PAD, :] = jnp.zeros((SSD_PAD, SSD_XBC), F32)
        state[...] = jnp.zeros(state.shape, F32)

    @pl.when(c > 0)
    def _():
        pad[0:SSD_PAD, :] = pad[SSD_STEP_ROWS:SSD_ROWS, :]

    pad[SSD_PAD:SSD_ROWS, 0:SSD_D_INNER] = x_ref[...].astype(F32)
    pad[SSD_PAD:SSD_ROWS, SSD_D_INNER:SSD_XBC] = bc_ref[...].astype(F32)

    for sub in range(SSD_STEP_ROWS // L):
        _ssd_scan_chunk(sub * L, z_ref, dtraw_ref, cw_ref, cbias_ref, dtb_ref, alog_ref,
                        dskip_ref, nw_ref, tril_ref, expand_ref, o_ref, pad, xc_s, b_s, c_s, state)


def _ssd_scan_chunk(r0, z_ref, dtraw_ref, cw_ref, cbias_ref, dtb_ref, alog_ref,
                    dskip_ref, nw_ref, tril_ref, expand_ref, o_ref, pad, xc_s, b_s, c_s, state):
    L = SSD_L
    rows = slice(r0, r0 + L)

    bc_width = SSD_GROUPS * SSD_STATE
    for p in range(SSD_XBC // SSD_CONV_PIECE):
        lo = p * SSD_CONV_PIECE
        cols = slice(lo, lo + SSD_CONV_PIECE)
        wh = 0.5 * cw_ref[:, cols]
        u = pad[r0:r0 + SSD_PAD + L, cols]
        u1 = pltpu.roll(u, 1, 0)
        older = u1 * wh[0:1, :] + u * wh[1:2, :]
        newer = u1 * wh[2:3, :] + u * wh[3:4, :]
        acc = pltpu.roll(older, 2, 0) + newer
        h = acc[SSD_PAD:SSD_PAD + L, :] + 0.5 * cbias_ref[:, cols]
        v = h + h * jnp.tanh(h)
        if lo < SSD_D_INNER:
            xc_s[:, cols] = v
        elif lo < SSD_D_INNER + bc_width:
            b_s[:, lo - SSD_D_INNER:lo - SSD_D_INNER + SSD_CONV_PIECE] = v.astype(BF16)
        else:
            off = lo - SSD_D_INNER - bc_width
            c_s[:, off:off + SSD_CONV_PIECE] = v.astype(BF16)

    raw = dtraw_ref[rows, :] + dtb_ref[...]
    dt = jnp.maximum(raw, 0.0) + jnp.log1p(jnp.exp(-jnp.abs(raw)))
    adt = dt * (-jnp.exp(alog_ref[...]) * LOG2E)
    tril = tril_ref[...]
    acum = sum(_dot(tril, p) for p in _split3(adt))
    rsrc_t = (acum - jnp.log(dt) * LOG2E).T
    ea_parts = _split2(jnp.exp2(acum))
    wst_parts = _split2(jnp.exp2(acum[L - 1:L, :] - acum) * dt)

    row = lax.broadcasted_iota(jnp.int32, (L, L), 0)
    col = lax.broadcasted_iota(jnp.int32, (L, L), 1)
    causal = col <= row
    low_half = lax.broadcasted_iota(jnp.int32, (L, LANES), 1) < SSD_HEAD_DIM
    for g in range(SSD_GROUPS):
        gl = slice(g * SSD_GROUP_WIDTH, (g + 1) * SSD_GROUP_WIDTH)
        nl = slice(g * SSD_STATE, (g + 1) * SSD_STATE)
        bg = b_s[:, nl]
        cg = c_s[:, nl]
        cb = _dot_nt(cg, bg)
        xg = xc_s[:, gl]
        halves = []
        for t in range(2):
            xt = xg[:, t * LANES:(t + 1) * LANES]
            acc = None
            for u in range(2):
                hd = g * SSD_HEADS_PER_GROUP + 2 * t + u
                seg = acum[:, hd:hd + 1] - rsrc_t[hd:hd + 1, :]
                m = (cb * jnp.where(causal, jnp.exp2(seg), 0.0)).astype(BF16)
                keep = low_half if u == 0 else jnp.logical_not(low_half)
                d = _dot(m, jnp.where(keep, xt, 0.0).astype(BF16))
                acc = d if acc is None else acc + d
            halves.append(acc)
        y = jnp.concatenate(halves, axis=1)

        expand = expand_ref[:, gl]
        ea_g = sum(_dot(p, expand) for p in ea_parts)
        wst_g = sum(_dot(p, expand) for p in wst_parts)
        st = state[g]
        y = y + _dot(cg, st.astype(BF16)) * ea_g
        state[g] = (st * ea_g[L - 1:L, :]
                    + _dot(bg.astype(F32).T.astype(BF16), (xg * wst_g).astype(BF16)))

        y = y + dskip_ref[:, gl] * xg
        gated = y * _silu(z_ref[rows, gl].astype(F32))
        ms = jnp.mean(gated * gated, axis=-1, keepdims=True)
        o_ref[rows, gl] = (gated * lax.rsqrt(ms + NORM_EPS) * nw_ref[:, gl]).astype(BF16)


def _ssd(proj, dt_raw, conv_w, conv_b, dt_bias, a_log, d_skip, ssd_norm_w):
    def pad_heads(p):
        return jnp.pad(p.reshape(1, SSD_HEADS), ((0, 0), (0, LANES - SSD_HEADS)))

    expand = np.zeros((LANES, SSD_D_INNER), np.float32)
    for hd in range(SSD_HEADS):
        expand[hd, hd * SSD_HEAD_DIM:(hd + 1) * SSD_HEAD_DIM] = 1.0
    tril = np.tril(np.ones((SSD_L, SSD_L), np.float32))
    dskip_x = jnp.repeat(d_skip, SSD_HEAD_DIM).reshape(1, SSD_D_INNER)

    w = SSD_D_INNER
    xb, bcb, zb = COL_SSD_X // w, COL_SSD_B // w, COL_SSD_Z // w
    rows = lambda b, c: b * SSD_NC + c
    full = lambda b, c: (0, 0)
    return pl.pallas_call(
        _ssd_kernel,
        grid=(BATCH, SSD_NC),
        in_specs=[
            pl.BlockSpec((SSD_STEP_ROWS, w), lambda b, c: (rows(b, c), xb)),
            pl.BlockSpec((SSD_STEP_ROWS, w), lambda b, c: (rows(b, c), bcb)),
            pl.BlockSpec((SSD_STEP_ROWS, w), lambda b, c: (rows(b, c), zb)),
            pl.BlockSpec((SSD_STEP_ROWS, LANES), lambda b, c: (rows(b, c), 0)),
            pl.BlockSpec((SSD_CONV, SSD_XBC), full),
            pl.BlockSpec((1, SSD_XBC), full),
            pl.BlockSpec((1, LANES), full),
            pl.BlockSpec((1, LANES), full),
            pl.BlockSpec((1, w), full),
            pl.BlockSpec((1, w), full),
            pl.BlockSpec((SSD_L, SSD_L), full),
            pl.BlockSpec((LANES, w), full),
        ],
        out_specs=pl.BlockSpec((SSD_STEP_ROWS, w), lambda b, c: (rows(b, c), 0)),
        out_shape=jax.ShapeDtypeStruct((TOKENS, SSD_D_INNER), BF16),
        scratch_shapes=[
            pltpu.VMEM((SSD_ROWS, SSD_XBC), F32),
            pltpu.VMEM((SSD_L, SSD_D_INNER), F32),
            pltpu.VMEM((SSD_L, SSD_GROUPS * SSD_STATE), BF16),
            pltpu.VMEM((SSD_L, SSD_GROUPS * SSD_STATE), BF16),
            pltpu.VMEM((SSD_GROUPS, SSD_STATE, SSD_GROUP_WIDTH), F32),
        ],
        compiler_params=pltpu.CompilerParams(
            dimension_semantics=("parallel", "arbitrary"), vmem_limit_bytes=VMEM_LIMIT),
        name="ssd",
    )(proj, proj, proj, dt_raw, conv_w, conv_b,
      pad_heads(dt_bias), pad_heads(a_log), dskip_x, ssd_norm_w.reshape(1, SSD_D_INNER),
      jnp.asarray(tril, dtype=BF16), jnp.asarray(expand, dtype=BF16))


MERGE_TM = 256


def _merge_out_kernel(x_ref, osb_ref, ossd_ref, omem_ref, gsb_ref, gssd_ref, gmem_ref, bg_ref,
                      wsb_ref, wssd_ref, wmem_ref, wout_ref, fnw_ref, o_ref):
    def gate(g_ref, k):
        return _sigmoid(g_ref[...].astype(F32) + bg_ref[:, k * D_MODEL:(k + 1) * D_MODEL])

    merged = gate(gsb_ref, 0) * _dot(osb_ref[...], wsb_ref[...])
    merged = merged + gate(gssd_ref, 1) * _dot(ossd_ref[...], wssd_ref[...])
    merged = merged + gate(gmem_ref, 2) * _dot(omem_ref[...], wmem_ref[...])
    y = x_ref[...] + _dot(merged.astype(BF16), wout_ref[...])
    ms = jnp.mean(y * y, axis=-1, keepdims=True)
    o_ref[...] = y * lax.rsqrt(ms + NORM_EPS) * fnw_ref[...]


def _merge_out(xt, o_sb, o_ssd, o_mem, proj, b_gate, w_sb, w_ssd, w_mem, w_out, final_norm_w):
    gb = COL_GATE // D_MODEL
    tile = lambda width: pl.BlockSpec((MERGE_TM, width), lambda i: (i, 0))
    resident = lambda shape: pl.BlockSpec(shape, lambda i: (0, 0), pipeline_mode=pl.Buffered(1))
    return pl.pallas_call(
        _merge_out_kernel,
        grid=(TOKENS // MERGE_TM,),
        in_specs=[
            tile(D_MODEL), tile(SB_WIDTH), tile(SSD_D_INNER), tile(MEM_WIDTH),
            pl.BlockSpec((MERGE_TM, D_MODEL), lambda i: (i, gb)),
            pl.BlockSpec((MERGE_TM, D_MODEL), lambda i: (i, gb + 1)),
            pl.BlockSpec((MERGE_TM, D_MODEL), lambda i: (i, gb + 2)),
            resident((1, 3 * D_MODEL)),
            resident((SB_WIDTH, D_MODEL)), resident((SSD_D_INNER, D_MODEL)),
            resident((MEM_WIDTH, D_MODEL)), resident((D_MODEL, D_MODEL)),
            resident((1, D_MODEL)),
        ],
        out_specs=tile(D_MODEL),
        out_shape=jax.ShapeDtypeStruct((TOKENS, D_MODEL), F32),
        compiler_params=pltpu.CompilerParams(
            dimension_semantics=("parallel",), vmem_limit_bytes=VMEM_LIMIT),
        name="merge_out",
    )(xt, o_sb, o_ssd, o_mem, proj, proj, proj, b_gate, w_sb, w_ssd, w_mem, w_out, final_norm_w)


MEMKV_TN = 512
MEM_TM = 1024


def _mem_kv_kernel(m_ref, nw_ref, w_ref, o_ref):
    m = m_ref[...]
    ms = jnp.mean(m * m, axis=-1, keepdims=True)
    mn = (m * lax.rsqrt(ms + NORM_EPS) * nw_ref[...]).astype(BF16)
    o_ref[...] = _dot(mn, w_ref[...]).astype(BF16)


def _mem_kv(mem2, mem_norm_w, w_kv):
    rows = BATCH * MEM_LEN
    return pl.pallas_call(
        _mem_kv_kernel,
        grid=(2 * MEM_WIDTH // MEMKV_TN,),
        in_specs=[
            pl.BlockSpec((rows, D_MODEL), lambda j: (0, 0)),
            pl.BlockSpec((1, D_MODEL), lambda j: (0, 0)),
            pl.BlockSpec((D_MODEL, MEMKV_TN), lambda j: (0, j)),
        ],
        out_specs=pl.BlockSpec((rows, MEMKV_TN), lambda j: (0, j)),
        out_shape=jax.ShapeDtypeStruct((rows, 2 * MEM_WIDTH), BF16),
        compiler_params=pltpu.CompilerParams(
            dimension_semantics=("parallel",), vmem_limit_bytes=VMEM_LIMIT),
        name="mem_kv",
    )(mem2, mem_norm_w, w_kv)


def _mem_attn_kernel(q_ref, z_ref, kv_ref, o_ref):
    scale = MEM_HEAD_DIM ** -0.5
    for hd in range(MEM_HEADS):
        lo, hi = hd * MEM_HEAD_DIM, (hd + 1) * MEM_HEAD_DIM
        s = _dot_nt(q_ref[:, lo:hi], kv_ref[:, lo:hi]) * scale
        p = jnp.exp(s - jnp.max(s, axis=-1, keepdims=True))
        den = jnp.sum(p, axis=-1, keepdims=True)
        o = _dot(p.astype(BF16), kv_ref[:, MEM_WIDTH + lo:MEM_WIDTH + hi]) / den
        z = z_ref[:, lo:hi].astype(F32)
        o_ref[:, lo:hi] = (o * _silu(z)).astype(BF16)


def _mem_attn(proj, kv):
    nt = SEQ // MEM_TM
    qb, zb = COL_MEM_Q // MEM_WIDTH, COL_MEM_Z // MEM_WIDTH
    return pl.pallas_call(
        _mem_attn_kernel,
        grid=(BATCH, nt),
        in_specs=[
            pl.BlockSpec((MEM_TM, MEM_WIDTH), lambda b, i: (b * nt + i, qb)),
            pl.BlockSpec((MEM_TM, MEM_WIDTH), lambda b, i: (b * nt + i, zb)),
            pl.BlockSpec((MEM_LEN, 2 * MEM_WIDTH), lambda b, i: (b, 0)),
        ],
        out_specs=pl.BlockSpec((MEM_TM, MEM_WIDTH), lambda b, i: (b * nt + i, 0)),
        out_shape=jax.ShapeDtypeStruct((TOKENS, MEM_WIDTH), BF16),
        compiler_params=pltpu.CompilerParams(
            dimension_semantics=("parallel", "parallel"), vmem_limit_bytes=VMEM_LIMIT),
        name="mem_attn",
    )(proj, proj, kv)


def kernel(x, mem, norm_w, mem_norm_w, w_in, b_gate, conv_w, conv_b, dt_bias, a_log, d_skip,
           ssd_norm_w, w_mem_kv, w_branch_sb, w_branch_ssd, w_branch_mem, w_out, final_norm_w):
    xt = x.reshape(TOKENS, D_MODEL)
    w_in_t = w_in[0].T

    h, dt_raw, proj = _norm_in(xt, norm_w[0].reshape(1, D_MODEL), w_in_t)
    proj, (w_kv, w_sb, w_ssd, w_mem, w_o) = _in_proj(
        h, w_in_t, proj, [w_mem_kv[0], w_branch_sb[0], w_branch_ssd[0], w_branch_mem[0], w_out[0]])
    o_sb = _sb_attn(proj)
    o_ssd = _ssd(proj, dt_raw, conv_w[0], conv_b[0].reshape(1, -1), dt_bias[0], a_log[0],
                 d_skip[0], ssd_norm_w[0])
    kv = _mem_kv(mem.reshape(BATCH * MEM_LEN, D_MODEL), mem_norm_w[0].reshape(1, D_MODEL), w_kv)
    o_mem = _mem_attn(proj, kv)
    out = _merge_out(xt, o_sb, o_ssd, o_mem, proj, b_gate[0].reshape(1, -1),
                     w_sb, w_ssd, w_mem, w_o, final_norm_w.reshape(1, D_MODEL))
    return out.reshape(BATCH, SEQ, D_MODEL)
```

```python
import numpy as np
import jax
import jax.numpy as jnp
from jax import lax
from jax.experimental import pallas as pl
from jax.experimental.pallas import tpu as pltpu

F32 = jnp.float32
BF16 = jnp.bfloat16

D_MODEL = 2048
BATCH = 2
SEQ = 4096
TOKENS = BATCH * SEQ

SB_HEADS = 8
SB_HEAD_DIM = 128
SB_WIDTH = SB_HEADS * SB_HEAD_DIM

SSD_D_INNER = 2048
SSD_HEAD_DIM = 64
SSD_HEADS = 32
SSD_GROUPS = 8
SSD_HEADS_PER_GROUP = 4
SSD_STATE = 128
SSD_CONV = 4
SSD_GROUP_WIDTH = SSD_HEADS_PER_GROUP * SSD_HEAD_DIM

MEM_LEN = 256
MEM_HEADS = 4
MEM_HEAD_DIM = 256
MEM_WIDTH = MEM_HEADS * MEM_HEAD_DIM

NORM_EPS = 1e-6

COL_SB_Q = 0
COL_SB_K = 1024
COL_SB_V = 2048
COL_SB_Z = 3072
COL_SSD_Z = 4096
COL_SSD_X = 6144
COL_SSD_B = 8192
COL_DT = 10240
DT_WIDTH = 32
COL_MEM_Q = 10240
COL_MEM_Z = 11264
COL_GATE = 12288
PROJ_WIDTH = 18432

LANES = 128
VMEM_LIMIT = 56 * 1024 * 1024


def _sigmoid(v):
    return 0.5 + 0.5 * jnp.tanh(0.5 * v)


def _silu(v):
    h = 0.5 * v
    return h + h * jnp.tanh(h)


def _split2(v):
    hi = v.astype(BF16)
    lo = (v - hi.astype(F32)).astype(BF16)
    return hi, lo


def _split3(v):
    hi = v.astype(BF16)
    r = v - hi.astype(F32)
    mid = r.astype(BF16)
    lo = (r - mid.astype(F32)).astype(BF16)
    return hi, mid, lo


def _dot(a, b):
    return jnp.dot(a, b, preferred_element_type=F32)


def _dot_nt(a, b):
    return lax.dot_general(a, b, (((1,), (1,)), ((), ())), preferred_element_type=F32)


NORM_TM = 1024
NORM_CHUNK = 256
PROJ_TM = 2048
PROJ_TN = 1024


def _norm_in_kernel(x_ref, nw_ref, wdt_ref, wq_ref, h_ref, dt_ref, q_ref, wqb_ref):
    @pl.when(pl.program_id(0) == 0)
    def _():
        wqb_ref[...] = wq_ref[...].astype(BF16)

    wdt = wdt_ref[...].astype(BF16)
    for r in range(NORM_TM // NORM_CHUNK):
        rows = slice(r * NORM_CHUNK, (r + 1) * NORM_CHUNK)
        x = x_ref[rows, :]
        ms = jnp.mean(x * x, axis=-1, keepdims=True)
        hb = (x * lax.rsqrt(ms + NORM_EPS) * nw_ref[...]).astype(BF16)
        h_ref[rows, :] = hb
        dt_ref[rows, :] = _dot_nt(hb, wdt)
        q_ref[rows, :] = (_dot_nt(hb, wqb_ref[...]) * SB_Q_PRESCALE).astype(BF16)


def _norm_in(xt, norm_w, w_in_t):
    assert SB_WIDTH == PROJ_TN and COL_SB_Q == 0
    return pl.pallas_call(
        _norm_in_kernel,
        grid=(TOKENS // NORM_TM,),
        in_specs=[
            pl.BlockSpec((NORM_TM, D_MODEL), lambda i: (i, 0)),
            pl.BlockSpec((1, D_MODEL), lambda i: (0, 0)),
            pl.BlockSpec((LANES, D_MODEL), lambda i: (COL_DT // LANES, 0)),
            pl.BlockSpec((PROJ_TN, D_MODEL), lambda i: (0, 0), pipeline_mode=pl.Buffered(1)),
        ],
        out_specs=[
            pl.BlockSpec((NORM_TM, D_MODEL), lambda i: (i, 0)),
            pl.BlockSpec((NORM_TM, LANES), lambda i: (i, 0)),
            pl.BlockSpec((NORM_TM, PROJ_TN), lambda i: (i, 0)),
        ],
        out_shape=[
            jax.ShapeDtypeStruct((TOKENS, D_MODEL), BF16),
            jax.ShapeDtypeStruct((TOKENS, LANES), F32),
            jax.ShapeDtypeStruct((TOKENS, PROJ_WIDTH), BF16),
        ],
        scratch_shapes=[pltpu.VMEM((PROJ_TN, D_MODEL), BF16)],
        compiler_params=pltpu.CompilerParams(
            dimension_semantics=("arbitrary",), vmem_limit_bytes=VMEM_LIMIT),
        name="norm_in",
    )(xt, norm_w, w_in_t, w_in_t)


PROJ_SHIFT_TILE = COL_DT // PROJ_TN
PROJ_CAST_SLICES = 64
PROJ_VMEM_LIMIT = 60 * 1024 * 1024


def _in_proj_kernel(h_ref, w_ref, wnext_ref, proj_in_ref, *rest):
    del proj_in_ref
    n_side = (len(rest) - 2) // 2
    side_in, o_ref, side_out, wb_ref = rest[:n_side], rest[n_side], rest[n_side + 1:-1], rest[-1]
    for src, dst in zip(side_in, side_out):
        dst[...] = src[...].astype(BF16)

    tile = pl.program_id(0) + 1

    @pl.when(pl.program_id(1) == 0)
    def _():
        @pl.when(tile < PROJ_SHIFT_TILE)
        def _():
            wb_ref[...] = w_ref[...].astype(BF16)

        @pl.when(tile >= PROJ_SHIFT_TILE)
        def _():
            keep = PROJ_TN - DT_WIDTH
            wb_ref[:keep, :] = w_ref[DT_WIDTH:, :].astype(BF16)
            wb_ref[keep:, :] = wnext_ref[...].astype(BF16)

    o_ref[...] = _dot_nt(h_ref[...], wb_ref[...]).astype(BF16)


def _in_proj(h, w_in_t, proj, side_weights):
    next_per_tile = PROJ_TN // DT_WIDTH
    steps_i = TOKENS // PROJ_TM
    grid = (PROJ_WIDTH // PROJ_TN - 1, steps_i)
    assert grid[0] * grid[1] >= PROJ_CAST_SLICES
    slice_map = lambda j, i: (jnp.minimum(j * steps_i + i, PROJ_CAST_SLICES - 1), 0)
    side_specs = [pl.BlockSpec((w.shape[0] // PROJ_CAST_SLICES, D_MODEL), slice_map)
                  for w in side_weights]
    outs = pl.pallas_call(
        _in_proj_kernel,
        grid=grid,
        in_specs=[
            pl.BlockSpec((PROJ_TM, D_MODEL), lambda j, i: (i, 0)),
            pl.BlockSpec((PROJ_TN, D_MODEL), lambda j, i: (j + 1, 0)),
            pl.BlockSpec((DT_WIDTH, D_MODEL), lambda j, i: ((j + 2) * next_per_tile, 0)),
            pl.BlockSpec(memory_space=pl.ANY),
        ] + side_specs,
        out_specs=[pl.BlockSpec((PROJ_TM, PROJ_TN), lambda j, i: (i, j + 1))] + side_specs,
        out_shape=[jax.ShapeDtypeStruct((TOKENS, PROJ_WIDTH), BF16)]
        + [jax.ShapeDtypeStruct(w.shape, BF16) for w in side_weights],
        scratch_shapes=[pltpu.VMEM((PROJ_TN, D_MODEL), BF16)],
        input_output_aliases={3: 0},
        compiler_params=pltpu.CompilerParams(
            dimension_semantics=("arbitrary", "arbitrary"), vmem_limit_bytes=PROJ_VMEM_LIMIT),
        name="in_proj",
    )(h, w_in_t, w_in_t, proj, *side_weights)
    return outs[0], outs[1:]


SB_TQ = 256
SB_TK = 256
SB_NQ = SEQ // SB_TQ
SB_HPS = 8
SB_STEP_WIDTH = SB_HPS * SB_HEAD_DIM
LOG2E = 1.4426950408889634
SB_Q_PRESCALE = -(SB_HEAD_DIM ** -0.5) * LOG2E
SB_DEAD_LOG2 = -140.0
SB_HEAD_ROWS = 160


def _sb_log_keep(ny):
    lse = jnp.log(1.0 + jnp.exp2(-jnp.abs(ny))) * LOG2E
    return jnp.minimum(ny, 0.0) - lse


def _sb_suffix(lk, wcs):
    hi, lo = _split2(lk)
    c = _dot(jnp.concatenate([hi, lo], axis=1), wcs)
    return c[:, :LANES], c[:, LANES:]


def _sb_weights(ny, wcs, carry, mask):
    lk = _sb_log_keep(ny)
    if mask is not None:
        lk = jnp.where(mask, lk, 0.0)
    s1, r1 = _sb_suffix(lk[:, LANES:], wcs)
    s0, r0 = _sb_suffix(lk[:, :LANES], wcs)
    base0 = carry + r1
    w = jnp.exp2(jnp.concatenate([s0 + base0, s1 + carry], axis=1) - ny)
    if mask is not None:
        w = jnp.where(mask, w, 0.0)
    return base0 + r0, w.astype(BF16)


def _sb_attn_kernel(q_ref, k_ref, v_ref, z_ref, wcs_ref, o_ref, carry_ref, acc_ref, ny_ref, w_ref):
    i = pl.program_id(2)
    wcs = wcs_ref[...]
    row = lax.broadcasted_iota(jnp.int32, (SB_TQ, SB_TK), 0)
    col = lax.broadcasted_iota(jnp.int32, (SB_TQ, SB_TK), 1)
    heads = [slice(hh * SB_HEAD_DIM, (hh + 1) * SB_HEAD_DIM) for hh in range(SB_HPS)]

    def block_start(kb):
        return pl.multiple_of(kb * SB_TK, SB_TK)

    def qk(kb, rows):
        for hh, sl in enumerate(heads):
            ny_ref[hh, :rows] = _dot_nt(q_ref[:rows, sl], k_ref[pl.ds(block_start(kb), SB_TK), sl])

    def pv(kb, rows):
        for hh, sl in enumerate(heads):
            acc_ref[hh, :rows] += _dot(w_ref[hh, :rows], v_ref[pl.ds(block_start(kb), SB_TK), sl])

    def weights(mask, rows):
        for hh in range(SB_HPS):
            carry_ref[hh, :rows], w_ref[hh, :rows] = _sb_weights(
                ny_ref[hh, :rows], wcs, carry_ref[hh, :rows], mask)

    def alive(lo, hi):
        top = carry_ref[0, lo:hi]
        for hh in range(1, SB_HPS):
            top = jnp.maximum(top, carry_ref[hh, lo:hi])
        return jnp.max(top) > SB_DEAD_LOG2

    def earlier_blocks(rows):
        qk(i - 1, rows)
        pv(i, SB_TQ)
        weights(None, rows)
        qk(jnp.maximum(i - 2, 0), rows)

        def body(state):
            last, _ = state
            cur = last - 1
            pv(last, rows)
            weights(None, rows)
            qk(jnp.maximum(cur - 1, 0), rows)
            return cur, jnp.logical_and(cur > 0, alive(0, rows))

        first = i - 1
        last, _ = lax.while_loop(lambda state: state[1], body,
                                 (first, jnp.logical_and(first > 0, alive(0, rows))))
        pv(last, rows)

    carry_ref[...] = jnp.zeros(carry_ref.shape, F32)
    acc_ref[...] = jnp.zeros(acc_ref.shape, F32)
    qk(i, SB_TQ)
    weights(col < row, SB_TQ)
    go = jnp.logical_and(i > 0, alive(0, SB_TQ))
    tail_alive = alive(SB_HEAD_ROWS, SB_TQ)

    @pl.when(jnp.logical_not(go))
    def _():
        pv(i, SB_TQ)

    @pl.when(jnp.logical_and(go, tail_alive))
    def _():
        earlier_blocks(SB_TQ)

    @pl.when(jnp.logical_and(go, jnp.logical_not(tail_alive)))
    def _():
        earlier_blocks(SB_HEAD_ROWS)

    for hh, sl in enumerate(heads):
        z = z_ref[:, sl].astype(F32)
        o_ref[:, sl] = (acc_ref[hh] * _silu(z)).astype(BF16)


def _sb_cumsum_matrix():
    j = np.arange(2 * LANES)[:, None] % LANES
    s = np.arange(2 * LANES)[None, :]
    m = np.where(s < LANES, (j >= s), True)
    return jnp.asarray(m.astype(np.float32), dtype=BF16)


def _sb_attn(proj):
    w = SB_STEP_WIDTH
    qb, kb, vb, zb = COL_SB_Q // w, COL_SB_K // w, COL_SB_V // w, COL_SB_Z // w
    return pl.pallas_call(
        _sb_attn_kernel,
        grid=(BATCH, SB_HEADS // SB_HPS, SB_NQ),
        in_specs=[
            pl.BlockSpec((SB_TQ, w), lambda b, h, i: (b * SB_NQ + i, qb + h)),
            pl.BlockSpec((SEQ, w), lambda b, h, i: (b, kb + h)),
            pl.BlockSpec((SEQ, w), lambda b, h, i: (b, vb + h)),
            pl.BlockSpec((SB_TQ, w), lambda b, h, i: (b * SB_NQ + i, zb + h)),
            pl.BlockSpec((2 * LANES, 2 * LANES), lambda b, h, i: (0, 0)),
        ],
        out_specs=pl.BlockSpec((SB_TQ, w), lambda b, h, i: (b * SB_NQ + i, h)),
        out_shape=jax.ShapeDtypeStruct((TOKENS, SB_WIDTH), BF16),
        scratch_shapes=[
            pltpu.VMEM((SB_HPS, SB_TQ, LANES), F32),
            pltpu.VMEM((SB_HPS, SB_TQ, LANES), F32),
            pltpu.VMEM((SB_HPS, SB_TQ, SB_TK), F32),
            pltpu.VMEM((SB_HPS, SB_TQ, SB_TK), BF16),
        ],
        compiler_params=pltpu.CompilerParams(
            dimension_semantics=("parallel", "parallel", "arbitrary"),
            vmem_limit_bytes=VMEM_LIMIT),
        name="sb_attn",
    )(proj, proj, proj, proj, _sb_cumsum_matrix())


SSD_L = 128
SSD_STEP_ROWS = 512
SSD_NC = SEQ // SSD_STEP_ROWS
SSD_PAD = 8
SSD_ROWS = SSD_PAD + SSD_STEP_ROWS
SSD_XBC = SSD_D_INNER + 2 * SSD_GROUPS * SSD_STATE
SSD_CONV_PIECE = 128


def _ssd_kernel(x_ref, bc_ref, z_ref, dtraw_ref, cw_ref, cbias_ref, dtb_ref, alog_ref,
                dskip_ref, nw_ref, tril_ref, expand_ref,
                o_ref, pad, xc_s, b_s, c_s, state):
    c = pl.program_id(1)
    L = SSD_L

    @pl.when(c == 0)
    def _():
        pad[0:SSD_PAD, :] = jnp.zeros((SSD_PAD, SSD_XBC), F32)
        state[...] = jnp.zeros(state.shape, F32)

    @pl.when(c > 0)
    def _():
        pad[0:SSD_PAD, :] = pad[SSD_STEP_ROWS:SSD_ROWS, :]

    pad[SSD_PAD:SSD_ROWS, 0:SSD_D_INNER] = x_ref[...].astype(F32)
    pad[SSD_PAD:SSD_ROWS, SSD_D_INNER:SSD_XBC] = bc_ref[...].astype(F32)

    for sub in range(SSD_STEP_ROWS // L):
        _ssd_scan_chunk(sub * L, z_ref, dtraw_ref, cw_ref, cbias_ref, dtb_ref, alog_ref,
                        dskip_ref, nw_ref, tril_ref, expand_ref, o_ref, pad, xc_s, b_s, c_s, state)


def _ssd_scan_chunk(r0, z_ref, dtraw_ref, cw_ref, cbias_ref, dtb_ref, alog_ref,
                    dskip_ref, nw_ref, tril_ref, expand_ref, o_ref, pad, xc_s, b_s, c_s, state):
    L = SSD_L
    rows = slice(r0, r0 + L)

    bc_width = SSD_GROUPS * SSD_STATE
    for p in range(SSD_XBC // SSD_CONV_PIECE):
        lo = p * SSD_CONV_PIECE
        cols = slice(lo, lo + SSD_CONV_PIECE)
        wh = 0.5 * cw_ref[:, cols]
        u = pad[r0:r0 + SSD_PAD + L, cols]
        u1 = pltpu.roll(u, 1, 0)
        older = u1 * wh[0:1, :] + u * wh[1:2, :]
        newer = u1 * wh[2:3, :] + u * wh[3:4, :]
        acc = pltpu.roll(older, 2, 0) + newer
        h = acc[SSD_PAD:SSD_PAD + L, :] + 0.5 * cbias_ref[:, cols]
        v = h + h * jnp.tanh(h)
        if lo < SSD_D_INNER:
            xc_s[:, cols] = v
        elif lo < SSD_D_INNER + bc_width:
            b_s[:, lo - SSD_D_INNER:lo - SSD_D_INNER + SSD_CONV_PIECE] = v.astype(BF16)
        else:
            off = lo - SSD_D_INNER - bc_width
            c_s[:, off:off + SSD_CONV_PIECE] = v.astype(BF16)

    raw = dtraw_ref[rows, :] + dtb_ref[...]
    dt = jnp.maximum(raw, 0.0) + jnp.log1p(jnp.exp(-jnp.abs(raw)))
    adt = dt * (-jnp.exp(alog_ref[...]) * LOG2E)
    tril = tril_ref[...]
    acum = sum(_dot(tril, p) for p in _split3(adt))
    rsrc_t = (acum - jnp.log(dt) * LOG2E).T
    ea_split = jnp.concatenate(_split2(jnp.exp2(acum)), axis=1)
    wst_split = jnp.concatenate(
        _split2(jnp.exp2(acum[L - 1:L, :] - acum) * dt), axis=1)

    row = lax.broadcasted_iota(jnp.int32, (L, L), 0)
    col = lax.broadcasted_iota(jnp.int32, (L, L), 1)
    causal = col <= row
    low_half = lax.broadcasted_iota(jnp.int32, (L, LANES), 1) < SSD_HEAD_DIM
    for g in range(SSD_GROUPS):
        gl = slice(g * SSD_GROUP_WIDTH, (g + 1) * SSD_GROUP_WIDTH)
        nl = slice(g * SSD_STATE, (g + 1) * SSD_STATE)
        bg = b_s[:, nl]
        cg = c_s[:, nl]
        cb = _dot_nt(cg, bg)
        xg = xc_s[:, gl]
        halves = []
        for t in range(2):
            xt = xg[:, t * LANES:(t + 1) * LANES]
            acc = None
            for u in range(2):
                hd = g * SSD_HEADS_PER_GROUP + 2 * t + u
                seg = acum[:, hd:hd + 1] - rsrc_t[hd:hd + 1, :]
                m = (cb * jnp.where(causal, jnp.exp2(seg), 0.0)).astype(BF16)
                keep = low_half if u == 0 else jnp.logical_not(low_half)
                d = _dot(m, jnp.where(keep, xt, 0.0).astype(BF16))
                acc = d if acc is None else acc + d
            halves.append(acc)
        y = jnp.concatenate(halves, axis=1)

        expand = expand_ref[:, gl]
        ea_g = _dot(ea_split, expand)
        wst_g = _dot(wst_split, expand)
        st = state[g]
        y = y + _dot(cg, st.astype(BF16)) * ea_g
        state[g] = (st * ea_g[L - 1:L, :]
                    + _dot(bg.astype(F32).T.astype(BF16), (xg * wst_g).astype(BF16)))

        y = y + dskip_ref[:, gl] * xg
        gated = y * _silu(z_ref[rows, gl].astype(F32))
        ms = jnp.mean(gated * gated, axis=-1, keepdims=True)
        o_ref[rows, gl] = (gated * lax.rsqrt(ms + NORM_EPS) * nw_ref[:, gl]).astype(BF16)


def _ssd(proj, dt_raw, conv_w, conv_b, dt_bias, a_log, d_skip, ssd_norm_w):
    def pad_heads(p):
        return jnp.pad(p.reshape(1, SSD_HEADS), ((0, 0), (0, LANES - SSD_HEADS)))

    expand = np.zeros((2 * LANES, SSD_D_INNER), np.float32)
    for hd in range(SSD_HEADS):
        expand[hd, hd * SSD_HEAD_DIM:(hd + 1) * SSD_HEAD_DIM] = 1.0
        expand[LANES + hd, hd * SSD_HEAD_DIM:(hd + 1) * SSD_HEAD_DIM] = 1.0
    tril = np.tril(np.ones((SSD_L, SSD_L), np.float32))
    dskip_x = jnp.repeat(d_skip, SSD_HEAD_DIM).reshape(1, SSD_D_INNER)

    w = SSD_D_INNER
    xb, bcb, zb = COL_SSD_X // w, COL_SSD_B // w, COL_SSD_Z // w
    rows = lambda b, c: b * SSD_NC + c
    full = lambda b, c: (0, 0)
    return pl.pallas_call(
        _ssd_kernel,
        grid=(BATCH, SSD_NC),
        in_specs=[
            pl.BlockSpec((SSD_STEP_ROWS, w), lambda b, c: (rows(b, c), xb)),
            pl.BlockSpec((SSD_STEP_ROWS, w), lambda b, c: (rows(b, c), bcb)),
            pl.BlockSpec((SSD_STEP_ROWS, w), lambda b, c: (rows(b, c), zb)),
            pl.BlockSpec((SSD_STEP_ROWS, LANES), lambda b, c: (rows(b, c), 0)),
            pl.BlockSpec((SSD_CONV, SSD_XBC), full),
            pl.BlockSpec((1, SSD_XBC), full),
            pl.BlockSpec((1, LANES), full),
            pl.BlockSpec((1, LANES), full),
            pl.BlockSpec((1, w), full),
            pl.BlockSpec((1, w), full),
            pl.BlockSpec((SSD_L, SSD_L), full),
            pl.BlockSpec((2 * LANES, w), full),
        ],
        out_specs=pl.BlockSpec((SSD_STEP_ROWS, w), lambda b, c: (rows(b, c), 0)),
        out_shape=jax.ShapeDtypeStruct((TOKENS, SSD_D_INNER), BF16),
        scratch_shapes=[
            pltpu.VMEM((SSD_ROWS, SSD_XBC), F32),
            pltpu.VMEM((SSD_L, SSD_D_INNER), F32),
            pltpu.VMEM((SSD_L, SSD_GROUPS * SSD_STATE), BF16),
            pltpu.VMEM((SSD_L, SSD_GROUPS * SSD_STATE), BF16),
            pltpu.VMEM((SSD_GROUPS, SSD_STATE, SSD_GROUP_WIDTH), F32),
        ],
        compiler_params=pltpu.CompilerParams(
            dimension_semantics=("parallel", "arbitrary"), vmem_limit_bytes=VMEM_LIMIT),
        name="ssd",
    )(proj, proj, proj, dt_raw, conv_w, conv_b,
      pad_heads(dt_bias), pad_heads(a_log), dskip_x, ssd_norm_w.reshape(1, SSD_D_INNER),
      jnp.asarray(tril, dtype=BF16), jnp.asarray(expand, dtype=BF16))


MERGE_TM = 256


def _merge_out_kernel(x_ref, osb_ref, ossd_ref, omem_ref, gsb_ref, gssd_ref, gmem_ref, bg_ref,
                      wsb_ref, wssd_ref, wmem_ref, wout_ref, fnw_ref, o_ref):
    def gate(g_ref, k):
        return _sigmoid(g_ref[...].astype(F32) + bg_ref[:, k * D_MODEL:(k + 1) * D_MODEL])

    merged = gate(gsb_ref, 0) * _dot(osb_ref[...], wsb_ref[...])
    merged = merged + gate(gssd_ref, 1) * _dot(ossd_ref[...], wssd_ref[...])
    merged = merged + gate(gmem_ref, 2) * _dot(omem_ref[...], wmem_ref[...])
    y = x_ref[...] + _dot(merged.astype(BF16), wout_ref[...])
    ms = jnp.mean(y * y, axis=-1, keepdims=True)
    o_ref[...] = y * lax.rsqrt(ms + NORM_EPS) * fnw_ref[...]


def _merge_out(xt, o_sb, o_ssd, o_mem, proj, b_gate, w_sb, w_ssd, w_mem, w_out, final_norm_w):
    gb = COL_GATE // D_MODEL
    tile = lambda width: pl.BlockSpec((MERGE_TM, width), lambda i: (i, 0))
    resident = lambda shape: pl.BlockSpec(shape, lambda i: (0, 0), pipeline_mode=pl.Buffered(1))
    return pl.pallas_call(
        _merge_out_kernel,
        grid=(TOKENS // MERGE_TM,),
        in_specs=[
            tile(D_MODEL), tile(SB_WIDTH), tile(SSD_D_INNER), tile(MEM_WIDTH),
            pl.BlockSpec((MERGE_TM, D_MODEL), lambda i: (i, gb)),
            pl.BlockSpec((MERGE_TM, D_MODEL), lambda i: (i, gb + 1)),
            pl.BlockSpec((MERGE_TM, D_MODEL), lambda i: (i, gb + 2)),
            resident((1, 3 * D_MODEL)),
            resident((SB_WIDTH, D_MODEL)), resident((SSD_D_INNER, D_MODEL)),
            resident((MEM_WIDTH, D_MODEL)), resident((D_MODEL, D_MODEL)),
            resident((1, D_MODEL)),
        ],
        out_specs=tile(D_MODEL),
        out_shape=jax.ShapeDtypeStruct((TOKENS, D_MODEL), F32),
        compiler_params=pltpu.CompilerParams(
            dimension_semantics=("parallel",), vmem_limit_bytes=VMEM_LIMIT),
        name="merge_out",
    )(xt, o_sb, o_ssd, o_mem, proj, proj, proj, b_gate, w_sb, w_ssd, w_mem, w_out, final_norm_w)


MEMKV_TN = 512
MEM_TM = 1024


def _mem_kv_kernel(m_ref, nw_ref, w_ref, o_ref):
    m = m_ref[...]
    ms = jnp.mean(m * m, axis=-1, keepdims=True)
    mn = (m * lax.rsqrt(ms + NORM_EPS) * nw_ref[...]).astype(BF16)
    o_ref[...] = _dot(mn, w_ref[...]).astype(BF16)


def _mem_kv(mem2, mem_norm_w, w_kv):
    rows = BATCH * MEM_LEN
    return pl.pallas_call(
        _mem_kv_kernel,
        grid=(2 * MEM_WIDTH // MEMKV_TN,),
        in_specs=[
            pl.BlockSpec((rows, D_MODEL), lambda j: (0, 0)),
            pl.BlockSpec((1, D_MODEL), lambda j: (0, 0)),
            pl.BlockSpec((D_MODEL, MEMKV_TN), lambda j: (0, j)),
        ],
        out_specs=pl.BlockSpec((rows, MEMKV_TN), lambda j: (0, j)),
        out_shape=jax.ShapeDtypeStruct((rows, 2 * MEM_WIDTH), BF16),
        compiler_params=pltpu.CompilerParams(
            dimension_semantics=("parallel",), vmem_limit_bytes=VMEM_LIMIT),
        name="mem_kv",
    )(mem2, mem_norm_w, w_kv)


def _mem_attn_kernel(q_ref, z_ref, kv_ref, o_ref):
    scale = MEM_HEAD_DIM ** -0.5
    for hd in range(MEM_HEADS):
        lo, hi = hd * MEM_HEAD_DIM, (hd + 1) * MEM_HEAD_DIM
        s = _dot_nt(q_ref[:, lo:hi], kv_ref[:, lo:hi]) * scale
        p = jnp.exp(s - jnp.max(s, axis=-1, keepdims=True))
        den = jnp.sum(p, axis=-1, keepdims=True)
        o = _dot(p.astype(BF16), kv_ref[:, MEM_WIDTH + lo:MEM_WIDTH + hi]) / den
        z = z_ref[:, lo:hi].astype(F32)
        o_ref[:, lo:hi] = (o * _silu(z)).astype(BF16)


def _mem_attn(proj, kv):
    nt = SEQ // MEM_TM
    qb, zb = COL_MEM_Q // MEM_WIDTH, COL_MEM_Z // MEM_WIDTH
    return pl.pallas_call(
        _mem_attn_kernel,
        grid=(BATCH, nt),
        in_specs=[
            pl.BlockSpec((MEM_TM, MEM_WIDTH), lambda b, i: (b * nt + i, qb)),
            pl.BlockSpec((MEM_TM, MEM_WIDTH), lambda b, i: (b * nt + i, zb)),
            pl.BlockSpec((MEM_LEN, 2 * MEM_WIDTH), lambda b, i: (b, 0)),
        ],
        out_specs=pl.BlockSpec((MEM_TM, MEM_WIDTH), lambda b, i: (b * nt + i, 0)),
        out_shape=jax.ShapeDtypeStruct((TOKENS, MEM_WIDTH), BF16),
        compiler_params=pltpu.CompilerParams(
            dimension_semantics=("parallel", "parallel"), vmem_limit_bytes=VMEM_LIMIT),
        name="mem_attn",
    )(proj, proj, kv)


def kernel(x, mem, norm_w, mem_norm_w, w_in, b_gate, conv_w, conv_b, dt_bias, a_log, d_skip,
           ssd_norm_w, w_mem_kv, w_branch_sb, w_branch_ssd, w_branch_mem, w_out, final_norm_w):
    xt = x.reshape(TOKENS, D_MODEL)
    w_in_t = w_in[0].T

    h, dt_raw, proj = _norm_in(xt, norm_w[0].reshape(1, D_MODEL), w_in_t)
    proj, (w_kv, w_sb, w_ssd, w_mem, w_o) = _in_proj(
        h, w_in_t, proj, [w_mem_kv[0], w_branch_sb[0], w_branch_ssd[0], w_branch_mem[0], w_out[0]])
    o_sb = _sb_attn(proj)
    o_ssd = _ssd(proj, dt_raw, conv_w[0], conv_b[0].reshape(1, -1), dt_bias[0], a_log[0],
                 d_skip[0], ssd_norm_w[0])
    kv = _mem_kv(mem.reshape(BATCH * MEM_LEN, D_MODEL), mem_norm_w[0].reshape(1, D_MODEL), w_kv)
    o_mem = _mem_attn(proj, kv)
    out = _merge_out(xt, o_sb, o_ssd, o_mem, proj, b_gate[0].reshape(1, -1),
                     w_sb, w_ssd, w_mem, w_o, final_norm_w.reshape(1, D_MODEL))
    return out.reshape(BATCH, SEQ, D_MODEL)
```

```python
import numpy as np
import jax
import jax.numpy as jnp
from jax import lax
from jax.experimental import pallas as pl
from jax.experimental.pallas import tpu as pltpu

F32 = jnp.float32
BF16 = jnp.bfloat16

D_MODEL = 2048
BATCH = 2
SEQ = 4096
TOKENS = BATCH * SEQ

SB_HEADS = 8
SB_HEAD_DIM = 128
SB_WIDTH = SB_HEADS * SB_HEAD_DIM

SSD_D_INNER = 2048
SSD_HEAD_DIM = 64
SSD_HEADS = 32
SSD_GROUPS = 8
SSD_HEADS_PER_GROUP = 4
SSD_STATE = 128
SSD_CONV = 4
SSD_GROUP_WIDTH = SSD_HEADS_PER_GROUP * SSD_HEAD_DIM

MEM_LEN = 256
MEM_HEADS = 4
MEM_HEAD_DIM = 256
MEM_WIDTH = MEM_HEADS * MEM_HEAD_DIM

NORM_EPS = 1e-6

COL_SB_Q = 0
COL_SB_K = 1024
COL_SB_V = 2048
COL_SB_Z = 3072
COL_SSD_Z = 4096
COL_SSD_X = 6144
COL_SSD_B = 8192
COL_DT = 10240
DT_WIDTH = 32
COL_MEM_Q = 10240
COL_MEM_Z = 11264
COL_GATE = 12288
PROJ_WIDTH = 18432

LANES = 128
VMEM_LIMIT = 56 * 1024 * 1024


def _sigmoid(v):
    return 0.5 + 0.5 * jnp.tanh(0.5 * v)


def _silu(v):
    h = 0.5 * v
    return h + h * jnp.tanh(h)


def _split2(v):
    hi = v.astype(BF16)
    lo = (v - hi.astype(F32)).astype(BF16)
    return hi, lo


def _split3(v):
    hi = v.astype(BF16)
    r = v - hi.astype(F32)
    mid = r.astype(BF16)
    lo = (r - mid.astype(F32)).astype(BF16)
    return hi, mid, lo


def _dot(a, b):
    return jnp.dot(a, b, preferred_element_type=F32)


def _dot_nt(a, b):
    return lax.dot_general(a, b, (((1,), (1,)), ((), ())), preferred_element_type=F32)


NORM_TM = 1024
NORM_CHUNK = 256
PROJ_TM = 2048
PROJ_TN = 1024


def _norm_in_kernel(x_ref, nw_ref, wdt_ref, wq_ref, h_ref, dt_ref, q_ref, wqb_ref):
    @pl.when(pl.program_id(0) == 0)
    def _():
        wqb_ref[...] = wq_ref[...].astype(BF16)

    wdt = wdt_ref[...].astype(BF16)
    for r in range(NORM_TM // NORM_CHUNK):
        rows = slice(r * NORM_CHUNK, (r + 1) * NORM_CHUNK)
        x = x_ref[rows, :]
        ms = jnp.mean(x * x, axis=-1, keepdims=True)
        hb = (x * lax.rsqrt(ms + NORM_EPS) * nw_ref[...]).astype(BF16)
        h_ref[rows, :] = hb
        dt_ref[rows, :] = _dot_nt(hb, wdt)
        q_ref[rows, :] = (_dot_nt(hb, wqb_ref[...]) * SB_Q_PRESCALE).astype(BF16)


def _norm_in(xt, norm_w, w_in_t):
    assert SB_WIDTH == PROJ_TN and COL_SB_Q == 0
    return pl.pallas_call(
        _norm_in_kernel,
        grid=(TOKENS // NORM_TM,),
        in_specs=[
            pl.BlockSpec((NORM_TM, D_MODEL), lambda i: (i, 0)),
            pl.BlockSpec((1, D_MODEL), lambda i: (0, 0)),
            pl.BlockSpec((LANES, D_MODEL), lambda i: (COL_DT // LANES, 0)),
            pl.BlockSpec((PROJ_TN, D_MODEL), lambda i: (0, 0), pipeline_mode=pl.Buffered(1)),
        ],
        out_specs=[
            pl.BlockSpec((NORM_TM, D_MODEL), lambda i: (i, 0)),
            pl.BlockSpec((NORM_TM, LANES), lambda i: (i, 0)),
            pl.BlockSpec((NORM_TM, PROJ_TN), lambda i: (i, 0)),
        ],
        out_shape=[
            jax.ShapeDtypeStruct((TOKENS, D_MODEL), BF16),
            jax.ShapeDtypeStruct((TOKENS, LANES), F32),
            jax.ShapeDtypeStruct((TOKENS, PROJ_WIDTH), BF16),
        ],
        scratch_shapes=[pltpu.VMEM((PROJ_TN, D_MODEL), BF16)],
        compiler_params=pltpu.CompilerParams(
            dimension_semantics=("arbitrary",), vmem_limit_bytes=VMEM_LIMIT),
        name="norm_in",
    )(xt, norm_w, w_in_t, w_in_t)


PROJ_SHIFT_TILE = COL_DT // PROJ_TN
PROJ_CAST_SLICES = 64
PROJ_VMEM_LIMIT = 60 * 1024 * 1024


def _in_proj_kernel(h_ref, w_ref, wnext_ref, proj_in_ref, *rest):
    del proj_in_ref
    n_side = (len(rest) - 2) // 2
    side_in, o_ref, side_out, wb_ref = rest[:n_side], rest[n_side], rest[n_side + 1:-1], rest[-1]
    for src, dst in zip(side_in, side_out):
        dst[...] = src[...].astype(BF16)

    tile = pl.program_id(0) + 1

    @pl.when(pl.program_id(1) == 0)
    def _():
        @pl.when(tile < PROJ_SHIFT_TILE)
        def _():
            wb_ref[...] = w_ref[...].astype(BF16)

        @pl.when(tile >= PROJ_SHIFT_TILE)
        def _():
            keep = PROJ_TN - DT_WIDTH
            wb_ref[:keep, :] = w_ref[DT_WIDTH:, :].astype(BF16)
            wb_ref[keep:, :] = wnext_ref[...].astype(BF16)

    o_ref[...] = _dot_nt(h_ref[...], wb_ref[...]).astype(BF16)


def _in_proj(h, w_in_t, proj, side_weights):
    next_per_tile = PROJ_TN // DT_WIDTH
    steps_i = TOKENS // PROJ_TM
    grid = (PROJ_WIDTH // PROJ_TN - 1, steps_i)
    assert grid[0] * grid[1] >= PROJ_CAST_SLICES
    slice_map = lambda j, i: (jnp.minimum(j * steps_i + i, PROJ_CAST_SLICES - 1), 0)
    side_specs = [pl.BlockSpec((w.shape[0] // PROJ_CAST_SLICES, D_MODEL), slice_map)
                  for w in side_weights]
    outs = pl.pallas_call(
        _in_proj_kernel,
        grid=grid,
        in_specs=[
            pl.BlockSpec((PROJ_TM, D_MODEL), lambda j, i: (i, 0)),
            pl.BlockSpec((PROJ_TN, D_MODEL), lambda j, i: (j + 1, 0)),
            pl.BlockSpec((DT_WIDTH, D_MODEL), lambda j, i: ((j + 2) * next_per_tile, 0)),
            pl.BlockSpec(memory_space=pl.ANY),
        ] + side_specs,
        out_specs=[pl.BlockSpec((PROJ_TM, PROJ_TN), lambda j, i: (i, j + 1))] + side_specs,
        out_shape=[jax.ShapeDtypeStruct((TOKENS, PROJ_WIDTH), BF16)]
        + [jax.ShapeDtypeStruct(w.shape, BF16) for w in side_weights],
        scratch_shapes=[pltpu.VMEM((PROJ_TN, D_MODEL), BF16)],
        input_output_aliases={3: 0},
        compiler_params=pltpu.CompilerParams(
            dimension_semantics=("arbitrary", "arbitrary"), vmem_limit_bytes=PROJ_VMEM_LIMIT),
        name="in_proj",
    )(h, w_in_t, w_in_t, proj, *side_weights)
    return outs[0], outs[1:]


SB_TQ = 256
SB_TK = 256
SB_NQ = SEQ // SB_TQ
SB_HPS = 8
SB_STEP_WIDTH = SB_HPS * SB_HEAD_DIM
LOG2E = 1.4426950408889634
SB_Q_PRESCALE = -(SB_HEAD_DIM ** -0.5) * LOG2E
SB_DEAD_LOG2 = -140.0
SB_HEAD_ROWS = 160


def _sb_log_keep(ny):
    lse = jnp.log(1.0 + jnp.exp2(-jnp.abs(ny))) * LOG2E
    return jnp.minimum(ny, 0.0) - lse


def _sb_suffix(lk, wcs):
    hi, lo = _split2(lk)
    c = _dot(jnp.concatenate([hi, lo], axis=1), wcs)
    return c[:, :LANES], c[:, LANES:]


def _sb_weights(ny, wcs, carry, mask):
    lk = _sb_log_keep(ny)
    if mask is not None:
        lk = jnp.where(mask, lk, 0.0)
    s1, r1 = _sb_suffix(lk[:, LANES:], wcs)
    s0, r0 = _sb_suffix(lk[:, :LANES], wcs)
    base0 = carry + r1
    w = jnp.exp2(jnp.concatenate([s0 + base0, s1 + carry], axis=1) - ny)
    if mask is not None:
        w = jnp.where(mask, w, 0.0)
    return base0 + r0, w.astype(BF16)


def _sb_attn_kernel(q_ref, k_ref, v_ref, z_ref, wcs_ref, o_ref, carry_ref, acc_ref, ny_ref, w_ref):
    i = pl.program_id(2)
    wcs = wcs_ref[...]
    row = lax.broadcasted_iota(jnp.int32, (SB_TQ, SB_TK), 0)
    col = lax.broadcasted_iota(jnp.int32, (SB_TQ, SB_TK), 1)
    heads = [slice(hh * SB_HEAD_DIM, (hh + 1) * SB_HEAD_DIM) for hh in range(SB_HPS)]

    def block_start(kb):
        return pl.multiple_of(kb * SB_TK, SB_TK)

    def qk(kb, rows):
        for hh, sl in enumerate(heads):
            ny_ref[hh, :rows] = _dot_nt(q_ref[:rows, sl], k_ref[pl.ds(block_start(kb), SB_TK), sl])

    def pv(kb, rows):
        for hh, sl in enumerate(heads):
            acc_ref[hh, :rows] += _dot(w_ref[hh, :rows], v_ref[pl.ds(block_start(kb), SB_TK), sl])

    def weights(mask, rows):
        for hh in range(SB_HPS):
            carry_ref[hh, :rows], w_ref[hh, :rows] = _sb_weights(
                ny_ref[hh, :rows], wcs, carry_ref[hh, :rows], mask)

    def alive(lo, hi):
        top = carry_ref[0, lo:hi]
        for hh in range(1, SB_HPS):
            top = jnp.maximum(top, carry_ref[hh, lo:hi])
        return jnp.max(top) > SB_DEAD_LOG2

    def earlier_blocks(rows):
        qk(i - 1, rows)
        pv(i, SB_TQ)
        weights(None, rows)

        def body(state):
            last, _ = state
            cur = last - 1
            qk(cur, rows)
            pv(last, rows)
            weights(None, rows)
            return cur, jnp.logical_and(cur > 0, alive(0, rows))

        first = i - 1
        last, _ = lax.while_loop(lambda state: state[1], body,
                                 (first, jnp.logical_and(first > 0, alive(0, rows))))
        pv(last, rows)

    carry_ref[...] = jnp.zeros(carry_ref.shape, F32)
    acc_ref[...] = jnp.zeros(acc_ref.shape, F32)
    qk(i, SB_TQ)
    weights(col < row, SB_TQ)
    go = jnp.logical_and(i > 0, alive(0, SB_TQ))
    tail_alive = alive(SB_HEAD_ROWS, SB_TQ)

    @pl.when(jnp.logical_not(go))
    def _():
        pv(i, SB_TQ)

    @pl.when(jnp.logical_and(go, tail_alive))
    def _():
        earlier_blocks(SB_TQ)

    @pl.when(jnp.logical_and(go, jnp.logical_not(tail_alive)))
    def _():
        earlier_blocks(SB_HEAD_ROWS)

    for hh, sl in enumerate(heads):
        z = z_ref[:, sl].astype(F32)
        o_ref[:, sl] = (acc_ref[hh] * _silu(z)).astype(BF16)


def _sb_cumsum_matrix():
    j = np.arange(2 * LANES)[:, None] % LANES
    s = np.arange(2 * LANES)[None, :]
    m = np.where(s < LANES, (j >= s), True)
    return jnp.asarray(m.astype(np.float32), dtype=BF16)


def _sb_attn(proj):
    w = SB_STEP_WIDTH
    qb, kb, vb, zb = COL_SB_Q // w, COL_SB_K // w, COL_SB_V // w, COL_SB_Z // w
    return pl.pallas_call(
        _sb_attn_kernel,
        grid=(BATCH, SB_HEADS // SB_HPS, SB_NQ),
        in_specs=[
            pl.BlockSpec((SB_TQ, w), lambda b, h, i: (b * SB_NQ + i, qb + h)),
            pl.BlockSpec((SEQ, w), lambda b, h, i: (b, kb + h)),
            pl.BlockSpec((SEQ, w), lambda b, h, i: (b, vb + h)),
            pl.BlockSpec((SB_TQ, w), lambda b, h, i: (b * SB_NQ + i, zb + h)),
            pl.BlockSpec((2 * LANES, 2 * LANES), lambda b, h, i: (0, 0)),
        ],
        out_specs=pl.BlockSpec((SB_TQ, w), lambda b, h, i: (b * SB_NQ + i, h)),
        out_shape=jax.ShapeDtypeStruct((TOKENS, SB_WIDTH), BF16),
        scratch_shapes=[
            pltpu.VMEM((SB_HPS, SB_TQ, LANES), F32),
            pltpu.VMEM((SB_HPS, SB_TQ, LANES), F32),
            pltpu.VMEM((SB_HPS, SB_TQ, SB_TK), F32),
            pltpu.VMEM((SB_HPS, SB_TQ, SB_TK), BF16),
        ],
        compiler_params=pltpu.CompilerParams(
            dimension_semantics=("parallel", "parallel", "arbitrary"),
            vmem_limit_bytes=VMEM_LIMIT),
        name="sb_attn",
    )(proj, proj, proj, proj, _sb_cumsum_matrix())


SSD_L = 128
SSD_STEP_ROWS = 512
SSD_NC = SEQ // SSD_STEP_ROWS
SSD_PAD = 8
SSD_ROWS = SSD_PAD + SSD_STEP_ROWS
SSD_XBC = SSD_D_INNER + 2 * SSD_GROUPS * SSD_STATE
SSD_CONV_PIECE = 512


def _ssd_kernel(x_ref, bc_ref, z_ref, dtraw_ref, cw_ref, cbias_ref, dtb_ref, alog_ref,
                dskip_ref, nw_ref, tril_ref, expand_ref,
                o_ref, pad, xc_s, b_s, c_s, state):
    c = pl.program_id(1)
    L = SSD_L

    @pl.when(c == 0)
    def _():
        pad[0:SSD_PAD, :] = jnp.zeros((SSD_PAD, SSD_XBC), F32)
        state[...] = jnp.zeros(state.shape, F32)

    @pl.when(c > 0)
    def _():
        pad[0:SSD_PAD, :] = pad[SSD_STEP_ROWS:SSD_ROWS, :]

    pad[SSD_PAD:SSD_ROWS, 0:SSD_D_INNER] = x_ref[...].astype(F32)
    pad[SSD_PAD:SSD_ROWS, SSD_D_INNER:SSD_XBC] = bc_ref[...].astype(F32)

    for sub in range(SSD_STEP_ROWS // L):
        _ssd_scan_chunk(sub * L, z_ref, dtraw_ref, cw_ref, cbias_ref, dtb_ref, alog_ref,
                        dskip_ref, nw_ref, tril_ref, expand_ref, o_ref, pad, xc_s, b_s, c_s, state)


def _ssd_scan_chunk(r0, z_ref, dtraw_ref, cw_ref, cbias_ref, dtb_ref, alog_ref,
                    dskip_ref, nw_ref, tril_ref, expand_ref, o_ref, pad, xc_s, b_s, c_s, state):
    L = SSD_L
    rows = slice(r0, r0 + L)

    bc_width = SSD_GROUPS * SSD_STATE
    for p in range(SSD_XBC // SSD_CONV_PIECE):
        lo = p * SSD_CONV_PIECE
        cols = slice(lo, lo + SSD_CONV_PIECE)
        wh = 0.5 * cw_ref[:, cols]
        u = pad[r0:r0 + SSD_PAD + L, cols]
        u1 = pltpu.roll(u, 1, 0)
        older = u1 * wh[0:1, :] + u * wh[1:2, :]
        newer = u1 * wh[2:3, :] + u * wh[3:4, :]
        acc = pltpu.roll(older, 2, 0) + newer
        h = acc[SSD_PAD:SSD_PAD + L, :] + 0.5 * cbias_ref[:, cols]
        v = h + h * jnp.tanh(h)
        if lo < SSD_D_INNER:
            xc_s[:, cols] = v
        elif lo < SSD_D_INNER + bc_width:
            b_s[:, lo - SSD_D_INNER:lo - SSD_D_INNER + SSD_CONV_PIECE] = v.astype(BF16)
        else:
            off = lo - SSD_D_INNER - bc_width
            c_s[:, off:off + SSD_CONV_PIECE] = v.astype(BF16)

    raw = dtraw_ref[rows, :] + dtb_ref[...]
    dt = jnp.maximum(raw, 0.0) + jnp.log1p(jnp.exp(-jnp.abs(raw)))
    adt = dt * (-jnp.exp(alog_ref[...]) * LOG2E)
    tril = tril_ref[...]
    acum = sum(_dot(tril, p) for p in _split3(adt))
    rsrc_t = (acum - jnp.log(dt) * LOG2E).T
    ea_split = jnp.concatenate(_split2(jnp.exp2(acum)), axis=1)
    wst_split = jnp.concatenate(
        _split2(jnp.exp2(acum[L - 1:L, :] - acum) * dt), axis=1)

    row = lax.broadcasted_iota(jnp.int32, (L, L), 0)
    col = lax.broadcasted_iota(jnp.int32, (L, L), 1)
    causal = col <= row
    low_half = lax.broadcasted_iota(jnp.int32, (L, LANES), 1) < SSD_HEAD_DIM
    for g in range(SSD_GROUPS):
        gl = slice(g * SSD_GROUP_WIDTH, (g + 1) * SSD_GROUP_WIDTH)
        nl = slice(g * SSD_STATE, (g + 1) * SSD_STATE)
        bg = b_s[:, nl]
        cg = c_s[:, nl]
        cb = _dot_nt(cg, bg)
        xg = xc_s[:, gl]
        halves = []
        for t in range(2):
            xt = xg[:, t * LANES:(t + 1) * LANES]
            acc = None
            for u in range(2):
                hd = g * SSD_HEADS_PER_GROUP + 2 * t + u
                seg = acum[:, hd:hd + 1] - rsrc_t[hd:hd + 1, :]
                m = (cb * jnp.where(causal, jnp.exp2(seg), 0.0)).astype(BF16)
                keep = low_half if u == 0 else jnp.logical_not(low_half)
                d = _dot(m, jnp.where(keep, xt, 0.0).astype(BF16))
                acc = d if acc is None else acc + d
            halves.append(acc)
        y = jnp.concatenate(halves, axis=1)

        expand = expand_ref[:, gl]
        ea_g = _dot(ea_split, expand)
        wst_g = _dot(wst_split, expand)
        st = state[g]
        y = y + _dot(cg, st.astype(BF16)) * ea_g
        state[g] = (st * ea_g[L - 1:L, :]
                    + _dot(bg.astype(F32).T.astype(BF16), (xg * wst_g).astype(BF16)))

        y = y + dskip_ref[:, gl] * xg
        gated = y * _silu(z_ref[rows, gl].astype(F32))
        ms = jnp.mean(gated * gated, axis=-1, keepdims=True)
        o_ref[rows, gl] = (gated * lax.rsqrt(ms + NORM_EPS) * nw_ref[:, gl]).astype(BF16)


def _ssd(proj, dt_raw, conv_w, conv_b, dt_bias, a_log, d_skip, ssd_norm_w):
    def pad_heads(p):
        return jnp.pad(p.reshape(1, SSD_HEADS), ((0, 0), (0, LANES - SSD_HEADS)))

    expand = np.zeros((2 * LANES, SSD_D_INNER), np.float32)
    for hd in range(SSD_HEADS):
        expand[hd, hd * SSD_HEAD_DIM:(hd + 1) * SSD_HEAD_DIM] = 1.0
        expand[LANES + hd, hd * SSD_HEAD_DIM:(hd + 1) * SSD_HEAD_DIM] = 1.0
    tril = np.tril(np.ones((SSD_L, SSD_L), np.float32))
    dskip_x = jnp.repeat(d_skip, SSD_HEAD_DIM).reshape(1, SSD_D_INNER)

    w = SSD_D_INNER
    xb, bcb, zb = COL_SSD_X // w, COL_SSD_B // w, COL_SSD_Z // w
    rows = lambda b, c: b * SSD_NC + c
    full = lambda b, c: (0, 0)
    return pl.pallas_call(
        _ssd_kernel,
        grid=(BATCH, SSD_NC),
        in_specs=[
            pl.BlockSpec((SSD_STEP_ROWS, w), lambda b, c: (rows(b, c), xb)),
            pl.BlockSpec((SSD_STEP_ROWS, w), lambda b, c: (rows(b, c), bcb)),
            pl.BlockSpec((SSD_STEP_ROWS, w), lambda b, c: (rows(b, c), zb)),
            pl.BlockSpec((SSD_STEP_ROWS, LANES), lambda b, c: (rows(b, c), 0)),
            pl.BlockSpec((SSD_CONV, SSD_XBC), full),
            pl.BlockSpec((1, SSD_XBC), full),
            pl.BlockSpec((1, LANES), full),
            pl.BlockSpec((1, LANES), full),
            pl.BlockSpec((1, w), full),
            pl.BlockSpec((1, w), full),
            pl.BlockSpec((SSD_L, SSD_L), full),
            pl.BlockSpec((2 * LANES, w), full),
        ],
        out_specs=pl.BlockSpec((SSD_STEP_ROWS, w), lambda b, c: (rows(b, c), 0)),
        out_shape=jax.ShapeDtypeStruct((TOKENS, SSD_D_INNER), BF16),
        scratch_shapes=[
            pltpu.VMEM((SSD_ROWS, SSD_XBC), F32),
            pltpu.VMEM((SSD_L, SSD_D_INNER), F32),
            pltpu.VMEM((SSD_L, SSD_GROUPS * SSD_STATE), BF16),
            pltpu.VMEM((SSD_L, SSD_GROUPS * SSD_STATE), BF16),
            pltpu.VMEM((SSD_GROUPS, SSD_STATE, SSD_GROUP_WIDTH), F32),
        ],
        compiler_params=pltpu.CompilerParams(
            dimension_semantics=("parallel", "arbitrary"), vmem_limit_bytes=VMEM_LIMIT),
        name="ssd",
    )(proj, proj, proj, dt_raw, conv_w, conv_b,
      pad_heads(dt_bias), pad_heads(a_log), dskip_x, ssd_norm_w.reshape(1, SSD_D_INNER),
      jnp.asarray(tril, dtype=BF16), jnp.asarray(expand, dtype=BF16))


MERGE_TM = 256


def _merge_out_kernel(x_ref, osb_ref, ossd_ref, omem_ref, gsb_ref, gssd_ref, gmem_ref, bg_ref,
                      wsb_ref, wssd_ref, wmem_ref, wout_ref, fnw_ref, o_ref):
    def gate(g_ref, k):
        return _sigmoid(g_ref[...].astype(F32) + bg_ref[:, k * D_MODEL:(k + 1) * D_MODEL])

    merged = gate(gsb_ref, 0) * _dot(osb_ref[...], wsb_ref[...])
    merged = merged + gate(gssd_ref, 1) * _dot(ossd_ref[...], wssd_ref[...])
    merged = merged + gate(gmem_ref, 2) * _dot(omem_ref[...], wmem_ref[...])
    y = x_ref[...] + _dot(merged.astype(BF16), wout_ref[...])
    ms = jnp.mean(y * y, axis=-1, keepdims=True)
    o_ref[...] = y * lax.rsqrt(ms + NORM_EPS) * fnw_ref[...]


def _merge_out(xt, o_sb, o_ssd, o_mem, proj, b_gate, w_sb, w_ssd, w_mem, w_out, final_norm_w):
    gb = COL_GATE // D_MODEL
    tile = lambda width: pl.BlockSpec((MERGE_TM, width), lambda i: (i, 0))
    resident = lambda shape: pl.BlockSpec(shape, lambda i: (0, 0), pipeline_mode=pl.Buffered(1))
    return pl.pallas_call(
        _merge_out_kernel,
        grid=(TOKENS // MERGE_TM,),
        in_specs=[
            tile(D_MODEL), tile(SB_WIDTH), tile(SSD_D_INNER), tile(MEM_WIDTH),
            pl.BlockSpec((MERGE_TM, D_MODEL), lambda i: (i, gb)),
            pl.BlockSpec((MERGE_TM, D_MODEL), lambda i: (i, gb + 1)),
            pl.BlockSpec((MERGE_TM, D_MODEL), lambda i: (i, gb + 2)),
            resident((1, 3 * D_MODEL)),
            resident((SB_WIDTH, D_MODEL)), resident((SSD_D_INNER, D_MODEL)),
            resident((MEM_WIDTH, D_MODEL)), resident((D_MODEL, D_MODEL)),
            resident((1, D_MODEL)),
        ],
        out_specs=tile(D_MODEL),
        out_shape=jax.ShapeDtypeStruct((TOKENS, D_MODEL), F32),
        compiler_params=pltpu.CompilerParams(
            dimension_semantics=("parallel",), vmem_limit_bytes=VMEM_LIMIT),
        name="merge_out",
    )(xt, o_sb, o_ssd, o_mem, proj, proj, proj, b_gate, w_sb, w_ssd, w_mem, w_out, final_norm_w)


MEMKV_TN = 512
MEM_TM = 1024


def _mem_kv_kernel(m_ref, nw_ref, w_ref, o_ref):
    m = m_ref[...]
    ms = jnp.mean(m * m, axis=-1, keepdims=True)
    mn = (m * lax.rsqrt(ms + NORM_EPS) * nw_ref[...]).astype(BF16)
    o_ref[...] = _dot(mn, w_ref[...]).astype(BF16)


def _mem_kv(mem2, mem_norm_w, w_kv):
    rows = BATCH * MEM_LEN
    return pl.pallas_call(
        _mem_kv_kernel,
        grid=(2 * MEM_WIDTH // MEMKV_TN,),
        in_specs=[
            pl.BlockSpec((rows, D_MODEL), lambda j: (0, 0)),
            pl.BlockSpec((1, D_MODEL), lambda j: (0, 0)),
            pl.BlockSpec((D_MODEL, MEMKV_TN), lambda j: (0, j)),
        ],
        out_specs=pl.BlockSpec((rows, MEMKV_TN), lambda j: (0, j)),
        out_shape=jax.ShapeDtypeStruct((rows, 2 * MEM_WIDTH), BF16),
        compiler_params=pltpu.CompilerParams(
            dimension_semantics=("parallel",), vmem_limit_bytes=VMEM_LIMIT),
        name="mem_kv",
    )(mem2, mem_norm_w, w_kv)


def _mem_attn_kernel(q_ref, z_ref, kv_ref, o_ref):
    scale = MEM_HEAD_DIM ** -0.5
    for hd in range(MEM_HEADS):
        lo, hi = hd * MEM_HEAD_DIM, (hd + 1) * MEM_HEAD_DIM
        s = _dot_nt(q_ref[:, lo:hi], kv_ref[:, lo:hi]) * scale
        p = jnp.exp(s - jnp.max(s, axis=-1, keepdims=True))
        den = jnp.sum(p, axis=-1, keepdims=True)
        o = _dot(p.astype(BF16), kv_ref[:, MEM_WIDTH + lo:MEM_WIDTH + hi]) / den
        z = z_ref[:, lo:hi].astype(F32)
        o_ref[:, lo:hi] = (o * _silu(z)).astype(BF16)


def _mem_attn(proj, kv):
    nt = SEQ // MEM_TM
    qb, zb = COL_MEM_Q // MEM_WIDTH, COL_MEM_Z // MEM_WIDTH
    return pl.pallas_call(
        _mem_attn_kernel,
        grid=(BATCH, nt),
        in_specs=[
            pl.BlockSpec((MEM_TM, MEM_WIDTH), lambda b, i: (b * nt + i, qb)),
            pl.BlockSpec((MEM_TM, MEM_WIDTH), lambda b, i: (b * nt + i, zb)),
            pl.BlockSpec((MEM_LEN, 2 * MEM_WIDTH), lambda b, i: (b, 0)),
        ],
        out_specs=pl.BlockSpec((MEM_TM, MEM_WIDTH), lambda b, i: (b * nt + i, 0)),
        out_shape=jax.ShapeDtypeStruct((TOKENS, MEM_WIDTH), BF16),
        compiler_params=pltpu.CompilerParams(
            dimension_semantics=("parallel", "parallel"), vmem_limit_bytes=VMEM_LIMIT),
        name="mem_attn",
    )(proj, proj, kv)


def kernel(x, mem, norm_w, mem_norm_w, w_in, b_gate, conv_w, conv_b, dt_bias, a_log, d_skip,
           ssd_norm_w, w_mem_kv, w_branch_sb, w_branch_ssd, w_branch_mem, w_out, final_norm_w):
    xt = x.reshape(TOKENS, D_MODEL)
    w_in_t = w_in[0].T

    h, dt_raw, proj = _norm_in(xt, norm_w[0].reshape(1, D_MODEL), w_in_t)
    proj, (w_kv, w_sb, w_ssd, w_mem, w_o) = _in_proj(
        h, w_in_t, proj, [w_mem_kv[0], w_branch_sb[0], w_branch_ssd[0], w_branch_mem[0], w_out[0]])
    o_sb = _sb_attn(proj)
    o_ssd = _ssd(proj, dt_raw, conv_w[0], conv_b[0].reshape(1, -1), dt_bias[0], a_log[0],
                 d_skip[0], ssd_norm_w[0])
    kv = _mem_kv(mem.reshape(BATCH * MEM_LEN, D_MODEL), mem_norm_w[0].reshape(1, D_MODEL), w_kv)
    o_mem = _mem_attn(proj, kv)
    out = _merge_out(xt, o_sb, o_ssd, o_mem, proj, b_gate[0].reshape(1, -1),
                     w_sb, w_ssd, w_mem, w_o, final_norm_w.reshape(1, D_MODEL))
    return out.reshape(BATCH, SEQ, D_MODEL)
```

```python
import numpy as np
import jax
import jax.numpy as jnp
from jax import lax
from jax.experimental import pallas as pl
from jax.experimental.pallas import tpu as pltpu

F32 = jnp.float32
BF16 = jnp.bfloat16

D_MODEL = 2048
BATCH = 2
SEQ = 4096
TOKENS = BATCH * SEQ

SB_HEADS = 8
SB_HEAD_DIM = 128
SB_WIDTH = SB_HEADS * SB_HEAD_DIM

SSD_D_INNER = 2048
SSD_HEAD_DIM = 64
SSD_HEADS = 32
SSD_GROUPS = 8
SSD_HEADS_PER_GROUP = 4
SSD_STATE = 128
SSD_CONV = 4
SSD_GROUP_WIDTH = SSD_HEADS_PER_GROUP * SSD_HEAD_DIM

MEM_LEN = 256
MEM_HEADS = 4
MEM_HEAD_DIM = 256
MEM_WIDTH = MEM_HEADS * MEM_HEAD_DIM

NORM_EPS = 1e-6

COL_SB_Q = 0
COL_SB_K = 1024
COL_SB_V = 2048
COL_SB_Z = 3072
COL_SSD_Z = 4096
COL_SSD_X = 6144
COL_SSD_B = 8192
COL_DT = 10240
DT_WIDTH = 32
COL_MEM_Q = 10240
COL_MEM_Z = 11264
COL_GATE = 12288
PROJ_WIDTH = 18432

LANES = 128
VMEM_LIMIT = 56 * 1024 * 1024


def _sigmoid(v):
    return 0.5 + 0.5 * jnp.tanh(0.5 * v)


def _silu(v):
    h = 0.5 * v
    return h + h * jnp.tanh(h)


def _split2(v):
    hi = v.astype(BF16)
    lo = (v - hi.astype(F32)).astype(BF16)
    return hi, lo


def _split3(v):
    hi = v.astype(BF16)
    r = v - hi.astype(F32)
    mid = r.astype(BF16)
    lo = (r - mid.astype(F32)).astype(BF16)
    return hi, mid, lo


def _dot(a, b):
    return jnp.dot(a, b, preferred_element_type=F32)


def _dot_nt(a, b):
    return lax.dot_general(a, b, (((1,), (1,)), ((), ())), preferred_element_type=F32)


NORM_TM = 1024
NORM_CHUNK = 256
PROJ_TM = 2048
PROJ_TN = 1024


def _norm_in_kernel(x_ref, nw_ref, wdt_ref, wq_ref, h_ref, dt_ref, q_ref, wqb_ref):
    @pl.when(pl.program_id(0) == 0)
    def _():
        wqb_ref[...] = wq_ref[...].astype(BF16)

    wdt = wdt_ref[...].astype(BF16)
    for r in range(NORM_TM // NORM_CHUNK):
        rows = slice(r * NORM_CHUNK, (r + 1) * NORM_CHUNK)
        x = x_ref[rows, :]
        ms = jnp.mean(x * x, axis=-1, keepdims=True)
        hb = (x * lax.rsqrt(ms + NORM_EPS) * nw_ref[...]).astype(BF16)
        h_ref[rows, :] = hb
        dt_ref[rows, :] = _dot_nt(hb, wdt)
        q_ref[rows, :] = (_dot_nt(hb, wqb_ref[...]) * SB_Q_PRESCALE).astype(BF16)


def _norm_in(xt, norm_w, w_in_t):
    assert SB_WIDTH == PROJ_TN and COL_SB_Q == 0
    return pl.pallas_call(
        _norm_in_kernel,
        grid=(TOKENS // NORM_TM,),
        in_specs=[
            pl.BlockSpec((NORM_TM, D_MODEL), lambda i: (i, 0)),
            pl.BlockSpec((1, D_MODEL), lambda i: (0, 0)),
            pl.BlockSpec((LANES, D_MODEL), lambda i: (COL_DT // LANES, 0)),
            pl.BlockSpec((PROJ_TN, D_MODEL), lambda i: (0, 0), pipeline_mode=pl.Buffered(1)),
        ],
        out_specs=[
            pl.BlockSpec((NORM_TM, D_MODEL), lambda i: (i, 0)),
            pl.BlockSpec((NORM_TM, LANES), lambda i: (i, 0)),
            pl.BlockSpec((NORM_TM, PROJ_TN), lambda i: (i, 0)),
        ],
        out_shape=[
            jax.ShapeDtypeStruct((TOKENS, D_MODEL), BF16),
            jax.ShapeDtypeStruct((TOKENS, LANES), F32),
            jax.ShapeDtypeStruct((TOKENS, PROJ_WIDTH), BF16),
        ],
        scratch_shapes=[pltpu.VMEM((PROJ_TN, D_MODEL), BF16)],
        compiler_params=pltpu.CompilerParams(
            dimension_semantics=("arbitrary",), vmem_limit_bytes=VMEM_LIMIT),
        name="norm_in",
    )(xt, norm_w, w_in_t, w_in_t)


PROJ_SHIFT_TILE = COL_DT // PROJ_TN
PROJ_CAST_SLICES = 64
PROJ_VMEM_LIMIT = 60 * 1024 * 1024


def _in_proj_kernel(h_ref, w_ref, wnext_ref, proj_in_ref, *rest):
    del proj_in_ref
    n_side = (len(rest) - 2) // 2
    side_in, o_ref, side_out, wb_ref = rest[:n_side], rest[n_side], rest[n_side + 1:-1], rest[-1]
    for src, dst in zip(side_in, side_out):
        dst[...] = src[...].astype(BF16)

    tile = pl.program_id(0) + 1

    @pl.when(pl.program_id(1) == 0)
    def _():
        @pl.when(tile < PROJ_SHIFT_TILE)
        def _():
            wb_ref[...] = w_ref[...].astype(BF16)

        @pl.when(tile >= PROJ_SHIFT_TILE)
        def _():
            keep = PROJ_TN - DT_WIDTH
            wb_ref[:keep, :] = w_ref[DT_WIDTH:, :].astype(BF16)
            wb_ref[keep:, :] = wnext_ref[...].astype(BF16)

    o_ref[...] = _dot_nt(h_ref[...], wb_ref[...]).astype(BF16)


def _in_proj(h, w_in_t, proj, side_weights):
    next_per_tile = PROJ_TN // DT_WIDTH
    steps_i = TOKENS // PROJ_TM
    grid = (PROJ_WIDTH // PROJ_TN - 1, steps_i)
    assert grid[0] * grid[1] >= PROJ_CAST_SLICES
    slice_map = lambda j, i: (jnp.minimum(j * steps_i + i, PROJ_CAST_SLICES - 1), 0)
    side_specs = [pl.BlockSpec((w.shape[0] // PROJ_CAST_SLICES, D_MODEL), slice_map)
                  for w in side_weights]
    outs = pl.pallas_call(
        _in_proj_kernel,
        grid=grid,
        in_specs=[
            pl.BlockSpec((PROJ_TM, D_MODEL), lambda j, i: (i, 0)),
            pl.BlockSpec((PROJ_TN, D_MODEL), lambda j, i: (j + 1, 0)),
            pl.BlockSpec((DT_WIDTH, D_MODEL), lambda j, i: ((j + 2) * next_per_tile, 0)),
            pl.BlockSpec(memory_space=pl.ANY),
        ] + side_specs,
        out_specs=[pl.BlockSpec((PROJ_TM, PROJ_TN), lambda j, i: (i, j + 1))] + side_specs,
        out_shape=[jax.ShapeDtypeStruct((TOKENS, PROJ_WIDTH), BF16)]
        + [jax.ShapeDtypeStruct(w.shape, BF16) for w in side_weights],
        scratch_shapes=[pltpu.VMEM((PROJ_TN, D_MODEL), BF16)],
        input_output_aliases={3: 0},
        compiler_params=pltpu.CompilerParams(
            dimension_semantics=("arbitrary", "arbitrary"), vmem_limit_bytes=PROJ_VMEM_LIMIT),
        name="in_proj",
    )(h, w_in_t, w_in_t, proj, *side_weights)
    return outs[0], outs[1:]


SB_TQ = 256
SB_TK = 256
SB_NQ = SEQ // SB_TQ
SB_HPS = 8
SB_STEP_WIDTH = SB_HPS * SB_HEAD_DIM
LOG2E = 1.4426950408889634
SB_Q_PRESCALE = -(SB_HEAD_DIM ** -0.5) * LOG2E
SB_DEAD_LOG2 = -140.0
SB_HEAD_ROWS = 160


def _sb_log_keep(ny):
    lse = jnp.log(1.0 + jnp.exp2(-jnp.abs(ny))) * LOG2E
    return jnp.minimum(ny, 0.0) - lse


def _sb_suffix(lk, wcs):
    hi, lo = _split2(lk)
    c = _dot(jnp.concatenate([hi, lo], axis=1), wcs)
    return c[:, :LANES], c[:, LANES:]


def _sb_weights(ny, wcs, carry, mask):
    lk = _sb_log_keep(ny)
    if mask is not None:
        lk = jnp.where(mask, lk, 0.0)
    s1, r1 = _sb_suffix(lk[:, LANES:], wcs)
    s0, r0 = _sb_suffix(lk[:, :LANES], wcs)
    base0 = carry + r1
    w = jnp.exp2(jnp.concatenate([s0 + base0, s1 + carry], axis=1) - ny)
    if mask is not None:
        w = jnp.where(mask, w, 0.0)
    return base0 + r0, w.astype(BF16)


def _sb_attn_kernel(q_ref, k_ref, v_ref, z_ref, wcs_ref, o_ref, carry_ref, acc_ref, ny_ref, w_ref):
    i = pl.program_id(2)
    wcs = wcs_ref[...]
    row = lax.broadcasted_iota(jnp.int32, (SB_TQ, SB_TK), 0)
    col = lax.broadcasted_iota(jnp.int32, (SB_TQ, SB_TK), 1)
    heads = [slice(hh * SB_HEAD_DIM, (hh + 1) * SB_HEAD_DIM) for hh in range(SB_HPS)]

    def block_start(kb):
        return pl.multiple_of(kb * SB_TK, SB_TK)

    def qk(kb, rows):
        for hh, sl in enumerate(heads):
            ny_ref[hh, :rows] = _dot_nt(q_ref[:rows, sl], k_ref[pl.ds(block_start(kb), SB_TK), sl])

    def pv(kb, rows):
        for hh, sl in enumerate(heads):
            acc_ref[hh, :rows] += _dot(w_ref[hh, :rows], v_ref[pl.ds(block_start(kb), SB_TK), sl])

    def weights(mask, rows):
        for hh in range(SB_HPS):
            carry_ref[hh, :rows], w_ref[hh, :rows] = _sb_weights(
                ny_ref[hh, :rows], wcs, carry_ref[hh, :rows], mask)

    def alive(lo, hi):
        top = carry_ref[0, lo:hi]
        for hh in range(1, SB_HPS):
            top = jnp.maximum(top, carry_ref[hh, lo:hi])
        return jnp.max(top) > SB_DEAD_LOG2

    def earlier_blocks(rows):
        qk(i - 1, rows)
        pv(i, SB_TQ)
        weights(None, rows)

        def body(state):
            last, _ = state
            cur = last - 1
            qk(cur, rows)
            pv(last, rows)
            weights(None, rows)
            return cur, jnp.logical_and(cur > 0, alive(0, rows))

        first = i - 1
        last, _ = lax.while_loop(lambda state: state[1], body,
                                 (first, jnp.logical_and(first > 0, alive(0, rows))))
        pv(last, rows)

    carry_ref[...] = jnp.zeros(carry_ref.shape, F32)
    acc_ref[...] = jnp.zeros(acc_ref.shape, F32)
    qk(i, SB_TQ)
    weights(col < row, SB_TQ)
    go = jnp.logical_and(i > 0, alive(0, SB_TQ))
    tail_alive = alive(SB_HEAD_ROWS, SB_TQ)

    @pl.when(jnp.logical_not(go))
    def _():
        pv(i, SB_TQ)

    @pl.when(jnp.logical_and(go, tail_alive))
    def _():
        earlier_blocks(SB_TQ)

    @pl.when(jnp.logical_and(go, jnp.logical_not(tail_alive)))
    def _():
        earlier_blocks(SB_HEAD_ROWS)

    for hh, sl in enumerate(heads):
        z = z_ref[:, sl].astype(F32)
        o_ref[:, sl] = (acc_ref[hh] * _silu(z)).astype(BF16)


def _sb_cumsum_matrix():
    j = np.arange(2 * LANES)[:, None] % LANES
    s = np.arange(2 * LANES)[None, :]
    m = np.where(s < LANES, (j >= s), True)
    return jnp.asarray(m.astype(np.float32), dtype=BF16)


def _sb_attn(proj):
    w = SB_STEP_WIDTH
    qb, kb, vb, zb = COL_SB_Q // w, COL_SB_K // w, COL_SB_V // w, COL_SB_Z // w
    return pl.pallas_call(
        _sb_attn_kernel,
        grid=(BATCH, SB_HEADS // SB_HPS, SB_NQ),
        in_specs=[
            pl.BlockSpec((SB_TQ, w), lambda b, h, i: (b * SB_NQ + i, qb + h)),
            pl.BlockSpec((SEQ, w), lambda b, h, i: (b, kb + h)),
            pl.BlockSpec((SEQ, w), lambda b, h, i: (b, vb + h)),
            pl.BlockSpec((SB_TQ, w), lambda b, h, i: (b * SB_NQ + i, zb + h)),
            pl.BlockSpec((2 * LANES, 2 * LANES), lambda b, h, i: (0, 0)),
        ],
        out_specs=pl.BlockSpec((SB_TQ, w), lambda b, h, i: (b * SB_NQ + i, h)),
        out_shape=jax.ShapeDtypeStruct((TOKENS, SB_WIDTH), BF16),
        scratch_shapes=[
            pltpu.VMEM((SB_HPS, SB_TQ, LANES), F32),
            pltpu.VMEM((SB_HPS, SB_TQ, LANES), F32),
            pltpu.VMEM((SB_HPS, SB_TQ, SB_TK), F32),
            pltpu.VMEM((SB_HPS, SB_TQ, SB_TK), BF16),
        ],
        compiler_params=pltpu.CompilerParams(
            dimension_semantics=("parallel", "parallel", "arbitrary"),
            vmem_limit_bytes=VMEM_LIMIT),
        name="sb_attn",
    )(proj, proj, proj, proj, _sb_cumsum_matrix())


SSD_L = 128
SSD_STEP_ROWS = 512
SSD_NC = SEQ // SSD_STEP_ROWS
SSD_PAD = 8
SSD_ROWS = SSD_PAD + SSD_STEP_ROWS
SSD_XBC = SSD_D_INNER + 2 * SSD_GROUPS * SSD_STATE
SSD_CONV_PIECE = 512


def _ssd_kernel(x_ref, bc_ref, z_ref, dtraw_ref, cw_ref, cbias_ref, dtb_ref, alog_ref,
                dskip_ref, nw_ref, tril_ref, expand_ref,
                o_ref, pad, xc_s, b_s, c_s, state):
    c = pl.program_id(1)
    L = SSD_L

    @pl.when(c == 0)
    def _():
        pad[0:SSD_PAD, :] = jnp.zeros((SSD_PAD, SSD_XBC), F32)
        state[...] = jnp.zeros(state.shape, F32)

    @pl.when(c > 0)
    def _():
        pad[0:SSD_PAD, :] = pad[SSD_STEP_ROWS:SSD_ROWS, :]

    pad[SSD_PAD:SSD_ROWS, 0:SSD_D_INNER] = x_ref[...].astype(F32)
    pad[SSD_PAD:SSD_ROWS, SSD_D_INNER:SSD_XBC] = bc_ref[...].astype(F32)

    for sub in range(SSD_STEP_ROWS // L):
        _ssd_scan_chunk(sub * L, z_ref, dtraw_ref, cw_ref, cbias_ref, dtb_ref, alog_ref,
                        dskip_ref, nw_ref, tril_ref, expand_ref, o_ref, pad, xc_s, b_s, c_s, state)


def _ssd_scan_chunk(r0, z_ref, dtraw_ref, cw_ref, cbias_ref, dtb_ref, alog_ref,
                    dskip_ref, nw_ref, tril_ref, expand_ref, o_ref, pad, xc_s, b_s, c_s, state):
    L = SSD_L
    rows = slice(r0, r0 + L)

    bc_width = SSD_GROUPS * SSD_STATE
    for p in range(SSD_XBC // SSD_CONV_PIECE):
        lo = p * SSD_CONV_PIECE
        cols = slice(lo, lo + SSD_CONV_PIECE)
        wh = 0.5 * cw_ref[:, cols]
        u = pad[r0:r0 + SSD_PAD + L, cols]
        u1 = pltpu.roll(u, 1, 0)
        older = u1 * wh[0:1, :] + u * wh[1:2, :]
        newer = u1 * wh[2:3, :] + u * wh[3:4, :]
        acc = pltpu.roll(older, 2, 0) + newer
        h = acc[SSD_PAD:SSD_PAD + L, :] + 0.5 * cbias_ref[:, cols]
        v = h + h * jnp.tanh(h)
        if lo < SSD_D_INNER:
            xc_s[:, cols] = v
        elif lo < SSD_D_INNER + bc_width:
            b_s[:, lo - SSD_D_INNER:lo - SSD_D_INNER + SSD_CONV_PIECE] = v.astype(BF16)
        else:
            off = lo - SSD_D_INNER - bc_width
            c_s[:, off:off + SSD_CONV_PIECE] = v.astype(BF16)

    raw = dtraw_ref[rows, :] + dtb_ref[...]
    dt = jnp.maximum(raw, 0.0) + jnp.log1p(jnp.exp(-jnp.abs(raw)))
    adt = dt * (-jnp.exp(alog_ref[...]) * LOG2E)
    tril = tril_ref[...]
    acum = sum(_dot(tril, p) for p in _split3(adt))
    rsrc_t = (acum - jnp.log(dt) * LOG2E).T
    ea_split = jnp.concatenate(_split2(jnp.exp2(acum)), axis=1)
    wst_split = jnp.concatenate(
        _split2(jnp.exp2(acum[L - 1:L, :] - acum) * dt), axis=1)

    row = lax.broadcasted_iota(jnp.int32, (L, L), 0)
    col = lax.broadcasted_iota(jnp.int32, (L, L), 1)
    causal = col <= row
    low_half = lax.broadcasted_iota(jnp.int32, (L, LANES), 1) < SSD_HEAD_DIM
    for g in range(SSD_GROUPS):
        gl = slice(g * SSD_GROUP_WIDTH, (g + 1) * SSD_GROUP_WIDTH)
        nl = slice(g * SSD_STATE, (g + 1) * SSD_STATE)
        bg = b_s[:, nl]
        cg = c_s[:, nl]
        cb = _dot_nt(cg, bg)
        xg = xc_s[:, gl]
        halves = []
        for t in range(2):
            xt = xg[:, t * LANES:(t + 1) * LANES]
            acc = None
            for u in range(2):
                hd = g * SSD_HEADS_PER_GROUP + 2 * t + u
                seg = acum[:, hd:hd + 1] - rsrc_t[hd:hd + 1, :]
                m = (cb * jnp.where(causal, jnp.exp2(seg), 0.0)).astype(BF16)
                keep = low_half if u == 0 else jnp.logical_not(low_half)
                d = _dot(m, jnp.where(keep, xt, 0.0).astype(BF16))
                acc = d if acc is None else acc + d
            halves.append(acc)
        y = jnp.concatenate(halves, axis=1)

        expand = expand_ref[:, gl]
        ea_g = _dot(ea_split, expand)
        wst_g = _dot(wst_split, expand)
        st = state[g]
        y = y + _dot(cg, st.astype(BF16)) * ea_g
        state[g] = (st * ea_g[L - 1:L, :]
                    + _dot(bg.astype(F32).T.astype(BF16), (xg * wst_g).astype(BF16)))

        y = y + dskip_ref[:, gl] * xg
        gated = y * _silu(z_ref[rows, gl].astype(F32))
        ms = jnp.mean(gated * gated, axis=-1, keepdims=True)
        o_ref[rows, gl] = (gated * lax.rsqrt(ms + NORM_EPS) * nw_ref[:, gl]).astype(BF16)


def _ssd(proj, dt_raw, conv_w, conv_b, dt_bias, a_log, d_skip, ssd_norm_w):
    def pad_heads(p):
        return jnp.pad(p.reshape(1, SSD_HEADS), ((0, 0), (0, LANES - SSD_HEADS)))

    expand = np.zeros((2 * LANES, SSD_D_INNER), np.float32)
    for hd in range(SSD_HEADS):
        expand[hd, hd * SSD_HEAD_DIM:(hd + 1) * SSD_HEAD_DIM] = 1.0
        expand[LANES + hd, hd * SSD_HEAD_DIM:(hd + 1) * SSD_HEAD_DIM] = 1.0
    tril = np.tril(np.ones((SSD_L, SSD_L), np.float32))
    dskip_x = jnp.repeat(d_skip, SSD_HEAD_DIM).reshape(1, SSD_D_INNER)

    w = SSD_D_INNER
    xb, bcb, zb = COL_SSD_X // w, COL_SSD_B // w, COL_SSD_Z // w
    rows = lambda b, c: b * SSD_NC + c
    full = lambda b, c: (0, 0)
    return pl.pallas_call(
        _ssd_kernel,
        grid=(BATCH, SSD_NC),
        in_specs=[
            pl.BlockSpec((SSD_STEP_ROWS, w), lambda b, c: (rows(b, c), xb)),
            pl.BlockSpec((SSD_STEP_ROWS, w), lambda b, c: (rows(b, c), bcb)),
            pl.BlockSpec((SSD_STEP_ROWS, w), lambda b, c: (rows(b, c), zb)),
            pl.BlockSpec((SSD_STEP_ROWS, LANES), lambda b, c: (rows(b, c), 0)),
            pl.BlockSpec((SSD_CONV, SSD_XBC), full),
            pl.BlockSpec((1, SSD_XBC), full),
            pl.BlockSpec((1, LANES), full),
            pl.BlockSpec((1, LANES), full),
            pl.BlockSpec((1, w), full),
            pl.BlockSpec((1, w), full),
            pl.BlockSpec((SSD_L, SSD_L), full),
            pl.BlockSpec((2 * LANES, w), full),
        ],
        out_specs=pl.BlockSpec((SSD_STEP_ROWS, w), lambda b, c: (rows(b, c), 0)),
        out_shape=jax.ShapeDtypeStruct((TOKENS, SSD_D_INNER), BF16),
        scratch_shapes=[
            pltpu.VMEM((SSD_ROWS, SSD_XBC), F32),
            pltpu.VMEM((SSD_L, SSD_D_INNER), F32),
            pltpu.VMEM((SSD_L, SSD_GROUPS * SSD_STATE), BF16),
            pltpu.VMEM((SSD_L, SSD_GROUPS * SSD_STATE), BF16),
            pltpu.VMEM((SSD_GROUPS, SSD_STATE, SSD_GROUP_WIDTH), F32),
        ],
        compiler_params=pltpu.CompilerParams(
            dimension_semantics=("parallel", "arbitrary"), vmem_limit_bytes=VMEM_LIMIT),
        name="ssd",
    )(proj, proj, proj, dt_raw, conv_w, conv_b,
      pad_heads(dt_bias), pad_heads(a_log), dskip_x, ssd_norm_w.reshape(1, SSD_D_INNER),
      jnp.asarray(tril, dtype=BF16), jnp.asarray(expand, dtype=BF16))


MERGE_TM = 256


def _merge_out_kernel(x_ref, osb_ref, ossd_ref, omem_ref, gsb_ref, gssd_ref, gmem_ref, bg_ref,
                      wsb_ref, wssd_ref, wmem_ref, wout_ref, fnw_ref, o_ref):
    def gate(g_ref, k):
        return _sigmoid(g_ref[...].astype(F32) + bg_ref[:, k * D_MODEL:(k + 1) * D_MODEL])

    merged = gate(gsb_ref, 0) * _dot(osb_ref[...], wsb_ref[...])
    merged = merged + gate(gssd_ref, 1) * _dot(ossd_ref[...], wssd_ref[...])
    merged = merged + gate(gmem_ref, 2) * _dot(omem_ref[...], wmem_ref[...])
    y = x_ref[...] + _dot(merged.astype(BF16), wout_ref[...])
    ms = jnp.mean(y * y, axis=-1, keepdims=True)
    o_ref[...] = y * lax.rsqrt(ms + NORM_EPS) * fnw_ref[...]


def _merge_out(xt, o_sb, o_ssd, o_mem, proj, b_gate, w_sb, w_ssd, w_mem, w_out, final_norm_w):
    gb = COL_GATE // D_MODEL
    tile = lambda width: pl.BlockSpec((MERGE_TM, width), lambda i: (i, 0))
    resident = lambda shape: pl.BlockSpec(shape, lambda i: (0, 0), pipeline_mode=pl.Buffered(1))
    return pl.pallas_call(
        _merge_out_kernel,
        grid=(TOKENS // MERGE_TM,),
        in_specs=[
            tile(D_MODEL), tile(SB_WIDTH), tile(SSD_D_INNER), tile(MEM_WIDTH),
            pl.BlockSpec((MERGE_TM, D_MODEL), lambda i: (i, gb)),
            pl.BlockSpec((MERGE_TM, D_MODEL), lambda i: (i, gb + 1)),
            pl.BlockSpec((MERGE_TM, D_MODEL), lambda i: (i, gb + 2)),
            resident((1, 3 * D_MODEL)),
            resident((SB_WIDTH, D_MODEL)), resident((SSD_D_INNER, D_MODEL)),
            resident((MEM_WIDTH, D_MODEL)), resident((D_MODEL, D_MODEL)),
            resident((1, D_MODEL)),
        ],
        out_specs=tile(D_MODEL),
        out_shape=jax.ShapeDtypeStruct((TOKENS, D_MODEL), F32),
        compiler_params=pltpu.CompilerParams(
            dimension_semantics=("parallel",), vmem_limit_bytes=VMEM_LIMIT),
        name="merge_out",
    )(xt, o_sb, o_ssd, o_mem, proj, proj, proj, b_gate, w_sb, w_ssd, w_mem, w_out, final_norm_w)


MEM_TM = 1024


def _mem_attn_kernel(q_ref, z_ref, m_ref, nw_ref, wkv_ref, o_ref, kv_ref):
    @pl.when(pl.program_id(1) == 0)
    def _():
        m = m_ref[...]
        ms = jnp.mean(m * m, axis=-1, keepdims=True)
        mn = (m * lax.rsqrt(ms + NORM_EPS) * nw_ref[...]).astype(BF16)
        kv_ref[...] = _dot(mn, wkv_ref[...]).astype(BF16)

    scale = MEM_HEAD_DIM ** -0.5
    for hd in range(MEM_HEADS):
        lo, hi = hd * MEM_HEAD_DIM, (hd + 1) * MEM_HEAD_DIM
        s = _dot_nt(q_ref[:, lo:hi], kv_ref[:, lo:hi]) * scale
        p = jnp.exp(s - jnp.max(s, axis=-1, keepdims=True))
        den = jnp.sum(p, axis=-1, keepdims=True)
        o = _dot(p.astype(BF16), kv_ref[:, MEM_WIDTH + lo:MEM_WIDTH + hi]) / den
        z = z_ref[:, lo:hi].astype(F32)
        o_ref[:, lo:hi] = (o * _silu(z)).astype(BF16)


def _mem_attn(proj, mem2, mem_norm_w, w_kv):
    nt = SEQ // MEM_TM
    qb, zb = COL_MEM_Q // MEM_WIDTH, COL_MEM_Z // MEM_WIDTH
    return pl.pallas_call(
        _mem_attn_kernel,
        grid=(BATCH, nt),
        in_specs=[
            pl.BlockSpec((MEM_TM, MEM_WIDTH), lambda b, i: (b * nt + i, qb)),
            pl.BlockSpec((MEM_TM, MEM_WIDTH), lambda b, i: (b * nt + i, zb)),
            pl.BlockSpec((MEM_LEN, D_MODEL), lambda b, i: (b, 0)),
            pl.BlockSpec((1, D_MODEL), lambda b, i: (0, 0)),
            pl.BlockSpec((D_MODEL, 2 * MEM_WIDTH), lambda b, i: (0, 0), pipeline_mode=pl.Buffered(1)),
        ],
        out_specs=pl.BlockSpec((MEM_TM, MEM_WIDTH), lambda b, i: (b * nt + i, 0)),
        out_shape=jax.ShapeDtypeStruct((TOKENS, MEM_WIDTH), BF16),
        scratch_shapes=[pltpu.VMEM((MEM_LEN, 2 * MEM_WIDTH), BF16)],
        compiler_params=pltpu.CompilerParams(
            dimension_semantics=("parallel", "arbitrary"), vmem_limit_bytes=VMEM_LIMIT),
        name="mem_attn",
    )(proj, proj, mem2, mem_norm_w, w_kv)


def kernel(x, mem, norm_w, mem_norm_w, w_in, b_gate, conv_w, conv_b, dt_bias, a_log, d_skip,
           ssd_norm_w, w_mem_kv, w_branch_sb, w_branch_ssd, w_branch_mem, w_out, final_norm_w):
    xt = x.reshape(TOKENS, D_MODEL)
    w_in_t = w_in[0].T

    h, dt_raw, proj = _norm_in(xt, norm_w[0].reshape(1, D_MODEL), w_in_t)
    proj, (w_kv, w_sb, w_ssd, w_mem, w_o) = _in_proj(
        h, w_in_t, proj, [w_mem_kv[0], w_branch_sb[0], w_branch_ssd[0], w_branch_mem[0], w_out[0]])
    o_sb = _sb_attn(proj)
    o_ssd = _ssd(proj, dt_raw, conv_w[0], conv_b[0].reshape(1, -1), dt_bias[0], a_log[0],
                 d_skip[0], ssd_norm_w[0])
    o_mem = _mem_attn(proj, mem.reshape(BATCH * MEM_LEN, D_MODEL),
                      mem_norm_w[0].reshape(1, D_MODEL), w_kv)
    out = _merge_out(xt, o_sb, o_ssd, o_mem, proj, b_gate[0].reshape(1, -1),
                     w_sb, w_ssd, w_mem, w_o, final_norm_w.reshape(1, D_MODEL))
    return out.reshape(BATCH, SEQ, D_MODEL)
```
